```python
import jax, jax.numpy as jnp
from jax import lax
import numpy as np

D_MODEL = 2048
BATCH = 1
SEQ = 8192
DEPTH = 2

CHUNK = 64
Q_BLOCK = 128
N_HEADS = 16
QK_NOPE_DIM = 128
QK_ROPE_DIM = 64
V_HEAD_DIM = 128
QK_HEAD_DIM = QK_NOPE_DIM + QK_ROPE_DIM
Q_LORA = 512
KV_LORA = 512
ROPE_THETA = 10000.0
D_CONV = D_MODEL
CONV_WIDTH = 31
D_FF_DENSE = 5632
N_EXPERTS = 8
TOP_K = 2
D_FF_EXPERT = 7168
N_DENSE = (DEPTH + 1) // 2
N_MOE = DEPTH // 2
DEEPNORM_ALPHA = (2.0 * DEPTH) ** 0.25
DEEPNORM_BETA = (8.0 * DEPTH) ** -0.25
LN_EPS = 1e-5
RMS_EPS = 1e-6
IN_COLS = Q_LORA + KV_LORA + QK_ROPE_DIM + 2 * D_CONV + 2 * D_MODEL
IN_OFFSETS = (
    Q_LORA,
    Q_LORA + KV_LORA,
    Q_LORA + KV_LORA + QK_ROPE_DIM,
    Q_LORA + KV_LORA + QK_ROPE_DIM + D_CONV,
    Q_LORA + KV_LORA + QK_ROPE_DIM + 2 * D_CONV,
    Q_LORA + KV_LORA + QK_ROPE_DIM + 2 * D_CONV + D_MODEL,
)

kernel_name = 'hybrid_conformer_mla_moe_deepnorm_adaln'


def layer_norm(x, gain=None, bias=None):
    xf = x.astype(jnp.float32)
    mu = jnp.mean(xf, axis=-1, keepdims=True)
    var = jnp.mean(jnp.square(xf - mu), axis=-1, keepdims=True)
    y = (xf - mu) * lax.rsqrt(var + LN_EPS)
    if gain is not None:
        y = y * gain.astype(jnp.float32) + bias.astype(jnp.float32)
    return y.astype(x.dtype)


def rms_norm(x, gain):
    xf = x.astype(jnp.float32)
    y = xf * lax.rsqrt(jnp.mean(jnp.square(xf), axis=-1, keepdims=True) + RMS_EPS)
    return (y * gain.astype(jnp.float32)).astype(x.dtype)


def rope_tables(positions):
    inv_freq = ROPE_THETA ** (-jnp.arange(0, QK_ROPE_DIM, 2, dtype=jnp.float32) / QK_ROPE_DIM)
    ang = positions.astype(jnp.float32)[..., None] * inv_freq
    return jnp.cos(ang), jnp.sin(ang)


def apply_rope(x, cos, sin):
    half = QK_ROPE_DIM // 2
    x1, x2 = x[..., :half], x[..., half:]
    cos = cos.astype(x.dtype)
    sin = sin.astype(x.dtype)
    return jnp.concatenate([x1 * cos - x2 * sin, x2 * cos + x1 * sin], axis=-1)


def conv_module(u_val, u_gate, w_dw, b_dw, ln_g, ln_b, w_pw):
    u = u_val * jax.nn.sigmoid(u_gate)
    u = lax.conv_general_dilated(
        u, w_dw[:, None, :].astype(u.dtype), window_strides=(1,),
        padding=[(CONV_WIDTH - 1, 0)],
        dimension_numbers=('NWC', 'WIO', 'NWC'),
        feature_group_count=D_CONV) + b_dw
    u = jax.nn.silu(layer_norm(u, ln_g, ln_b))
    return u @ w_pw


def mla(c_q, c_kv, k_rope_in, cos, sin, g_q, w_uq, g_kv, w_ukv, w_o):
    B, S, _ = c_q.shape
    nb = S // Q_BLOCK
    q = (rms_norm(c_q, g_q) @ w_uq).reshape(B, S, N_HEADS, QK_HEAD_DIM)
    q_nope = q[..., :QK_NOPE_DIM]
    q_rope = apply_rope(q[..., QK_NOPE_DIM:], cos[:, :, None, :], sin[:, :, None, :])
    kv = (rms_norm(c_kv, g_kv) @ w_ukv).reshape(B, S, N_HEADS, QK_NOPE_DIM + V_HEAD_DIM)
    k_nope = kv[..., :QK_NOPE_DIM].transpose(0, 2, 1, 3)
    v = kv[..., QK_NOPE_DIM:].transpose(0, 2, 1, 3)
    k_rope = apply_rope(k_rope_in, cos, sin)
    scale = QK_HEAD_DIM ** -0.5
    qn_blocks = q_nope.reshape(B, nb, Q_BLOCK, N_HEADS, QK_NOPE_DIM).transpose(1, 0, 2, 3, 4)
    qr_blocks = q_rope.reshape(B, nb, Q_BLOCK, N_HEADS, QK_ROPE_DIM).transpose(1, 0, 2, 3, 4)
    chunk_k = jnp.arange(S) // CHUNK

    def attend(args):
        i, qn, qr = args
        s = (jnp.einsum('bqhd,bhkd->bhqk', qn, k_nope, preferred_element_type=jnp.float32)
             + jnp.einsum('bqhd,bkd->bhqk', qr, k_rope, preferred_element_type=jnp.float32))
        chunk_q = (i * Q_BLOCK + jnp.arange(Q_BLOCK)) // CHUNK
        mask = chunk_k[None, :] <= chunk_q[:, None]
        p = jax.nn.softmax(jnp.where(mask, s * scale, -jnp.inf), axis=-1)
        return jnp.einsum('bhqk,bhkd->bqhd', p.astype(v.dtype), v)

    o = lax.map(attend, (jnp.arange(nb), qn_blocks, qr_blocks))
    o = o.transpose(1, 0, 2, 3, 4).reshape(B, S, N_HEADS * V_HEAD_DIM)
    return o @ w_o


def swiglu(h, w_gate, w_up, w_down):
    return (jax.nn.silu(h @ w_gate) * (h @ w_up)) @ w_down


def moe_ffn(h, w_router, w_gate, w_up, w_down):
    B, S, D = h.shape
    t = h.reshape(B * S, D)
    logits = (t @ w_router).astype(jnp.float32)
    top_val, top_idx = lax.top_k(logits, TOP_K)
    top_w = jax.nn.softmax(top_val, axis=-1)
    comb = jnp.sum(jax.nn.one_hot(top_idx, N_EXPERTS, dtype=jnp.float32) * top_w[..., None], axis=1)
    comb = comb.astype(t.dtype)
    y = jnp.zeros_like(t)
    for e in range(N_EXPERTS):
        y = y + comb[:, e:e + 1] * swiglu(t, w_gate[e], w_up[e], w_down[e])
    return y.reshape(B, S, D)


def setup_inputs(seed: int = 0) -> dict:
    key = jax.random.key(seed)
    ks = iter(jax.random.split(key, 40))
    f32 = jnp.float32

    def nrm(shape, scale):
        return jax.random.normal(next(ks), shape, f32) * scale

    def gain(shape):
        return 1.0 + nrm(shape, 0.01)

    offsets = jax.random.randint(next(ks), (BATCH, 1), 0, 64) * CHUNK
    positions = (jnp.arange(SEQ, dtype=jnp.int32)[None, :] + offsets).astype(jnp.int32)
    return {
        'x': nrm((BATCH, SEQ, D_MODEL), 1.0),
        'c': nrm((BATCH, D_MODEL), 1.0),
        'positions': positions,
        'w_ada': nrm((DEPTH, D_MODEL, 6 * D_MODEL), 0.2 * D_MODEL ** -0.5),
        'b_ada': nrm((DEPTH, 6 * D_MODEL), 0.01),
        'w_in': nrm((DEPTH, D_MODEL, IN_COLS), D_MODEL ** -0.5),
        'g_q': gain((DEPTH, Q_LORA)),
        'w_uq': nrm((DEPTH, Q_LORA, N_HEADS * QK_HEAD_DIM), Q_LORA ** -0.5),
        'g_kv': gain((DEPTH, KV_LORA)),
        'w_ukv': nrm((DEPTH, KV_LORA, N_HEADS * (QK_NOPE_DIM + V_HEAD_DIM)), KV_LORA ** -0.5),
        'w_o': nrm((DEPTH, N_HEADS * V_HEAD_DIM, D_MODEL), (N_HEADS * V_HEAD_DIM) ** -0.5),
        'w_dw': nrm((DEPTH, CONV_WIDTH, D_CONV), CONV_WIDTH ** -0.5),
        'b_dw': nrm((DEPTH, D_CONV), 0.01),
        'conv_ln_g': gain((DEPTH, D_CONV)),
        'conv_ln_b': nrm((DEPTH, D_CONV), 0.01),
        'w_pw': nrm((DEPTH, D_CONV, D_MODEL), D_CONV ** -0.5),
        'w_out': nrm((DEPTH, D_MODEL, D_MODEL), DEEPNORM_BETA * D_MODEL ** -0.5),
        'ln1_g': gain((DEPTH, D_MODEL)),
        'ln1_b': nrm((DEPTH, D_MODEL), 0.01),
        'ln2_g': gain((DEPTH, D_MODEL)),
        'ln2_b': nrm((DEPTH, D_MODEL), 0.01),
        'w_ff_gate': nrm((N_DENSE, D_MODEL, D_FF_DENSE), D_MODEL ** -0.5),
        'w_ff_up': nrm((N_DENSE, D_MODEL, D_FF_DENSE), D_MODEL ** -0.5),
        'w_ff_down': nrm((N_DENSE, D_FF_DENSE, D_MODEL), DEEPNORM_BETA * D_FF_DENSE ** -0.5),
        'w_router': nrm((N_MOE, D_MODEL, N_EXPERTS), D_MODEL ** -0.5),
        'w_e_gate': nrm((N_MOE, N_EXPERTS, D_MODEL, D_FF_EXPERT), D_MODEL ** -0.5),
        'w_e_up': nrm((N_MOE, N_EXPERTS, D_MODEL, D_FF_EXPERT), D_MODEL ** -0.5),
        'w_e_down': nrm((N_MOE, N_EXPERTS, D_FF_EXPERT, D_MODEL), DEEPNORM_BETA * D_FF_EXPERT ** -0.5),
    }


def reference(x, c, positions, w_ada, b_ada, w_in, g_q, w_uq, g_kv, w_ukv, w_o,
              w_dw, b_dw, conv_ln_g, conv_ln_b, w_pw, w_out, ln1_g, ln1_b, ln2_g, ln2_b,
              w_ff_gate, w_ff_up, w_ff_down, w_router, w_e_gate, w_e_up, w_e_down):
    cos, sin = rope_tables(positions)
    c_act = jax.nn.silu(c)
    for l in range(DEPTH):
        mod = (c_act @ w_ada[l] + b_ada[l])[:, None, :]
        sh1, sc1, g1, sh2, sc2, g2 = jnp.split(mod, 6, axis=-1)

        h = layer_norm(x) * (1.0 + sc1) + sh1
        z = h @ w_in[l]
        c_q, c_kv, k_r, u_val, u_gate, gl_conv, gl_mla = jnp.split(z, IN_OFFSETS, axis=-1)
        y_conv = conv_module(u_val, u_gate, w_dw[l], b_dw[l], conv_ln_g[l], conv_ln_b[l], w_pw[l])
        y_mla = mla(c_q, c_kv, k_r, cos, sin, g_q[l], w_uq[l], g_kv[l], w_ukv[l], w_o[l])
        mixed = jax.nn.sigmoid(gl_conv) * y_conv + jax.nn.sigmoid(gl_mla) * y_mla
        x = layer_norm(DEEPNORM_ALPHA * x + (1.0 + g1) * (mixed @ w_out[l]), ln1_g[l], ln1_b[l])

        h = layer_norm(x) * (1.0 + sc2) + sh2
        if l % 2 == 0:
            i = l // 2
            f = swiglu(h, w_ff_gate[i], w_ff_up[i], w_ff_down[i])
        else:
            i = l // 2
            f = moe_ffn(h, w_router[i], w_e_gate[i], w_e_up[i], w_e_down[i])
        x = layer_norm(DEEPNORM_ALPHA * x + (1.0 + g2) * f, ln2_g[l], ln2_b[l])
    return x
```

```python
import functools

import jax
import jax.numpy as jnp
from jax import lax
from jax.experimental import pallas as pl
from jax.experimental.pallas import tpu as pltpu

F32 = jnp.float32
BF16 = jnp.bfloat16

D_MODEL = 2048
SEQ = 8192
DEPTH = 2
CHUNK = 64
N_HEADS = 16
QK_NOPE_DIM = 128
QK_ROPE_DIM = 64
V_HEAD_DIM = 128
QK_HEAD_DIM = QK_NOPE_DIM + QK_ROPE_DIM
Q_LORA = 512
KV_LORA = 512
ROPE_THETA = 10000.0
CONV_WIDTH = 31
D_FF_DENSE = 5632
N_EXPERTS = 8
D_FF_EXPERT = 7168
DEEPNORM_ALPHA = (2.0 * DEPTH) ** 0.25
LN_EPS = 1e-5
RMS_EPS = 1e-6
LATENT_COLS = Q_LORA + KV_LORA + 2 * QK_ROPE_DIM
CONV_HALO = 32
LANES = 128

VMEM_LIMIT = 48 * 1024 * 1024


def _params(*sem):
    return pltpu.CompilerParams(dimension_semantics=sem, vmem_limit_bytes=VMEM_LIMIT)


def _layer_norm_rows(x):
    mu = jnp.mean(x, axis=-1, keepdims=True)
    xc = x - mu
    var = jnp.mean(xc * xc, axis=-1, keepdims=True)
    return xc * lax.rsqrt(var + LN_EPS)


def _sigmoid(x):
    return 1.0 / (1.0 + jnp.exp(-x))


def _ada_kernel(c_ref, w_ref, b_ref, o_ref):
    c = c_ref[...]
    c_act = c * _sigmoid(c)
    o_ref[0] = jnp.sum(c_act * w_ref[0], axis=0, keepdims=True) + b_ref[0]


def _ada(c_col, w_ada, b_ada, tn=1024):
    depth, d, n = w_ada.shape
    return pl.pallas_call(
        _ada_kernel,
        grid=(depth, n // tn),
        in_specs=[
            pl.BlockSpec((d, 1), lambda l, j: (0, 0)),
            pl.BlockSpec((1, d, tn), lambda l, j: (l, 0, j)),
            pl.BlockSpec((1, 1, tn), lambda l, j: (l, 0, j)),
        ],
        out_specs=pl.BlockSpec((1, 1, tn), lambda l, j: (l, 0, j)),
        out_shape=jax.ShapeDtypeStruct((depth, 1, n), F32),
        compiler_params=_params("parallel", "parallel"),
        name="ada_mod",
    )(c_col, w_ada, b_ada.reshape(depth, 1, n))


def _rope_kernel(pos_ref, inv_ref, cos_ref, sin_ref):
    ang = pos_ref[...].astype(F32) * inv_ref[...]
    cos_ref[...] = jnp.cos(ang)
    sin_ref[...] = jnp.sin(ang)


def _rope_tables(pos_col, inv_freq2, ts=512):
    s = pos_col.shape[0]
    r = inv_freq2.shape[1]
    return pl.pallas_call(
        _rope_kernel,
        grid=(s // ts,),
        in_specs=[pl.BlockSpec((ts, 1), lambda i: (i, 0)), pl.BlockSpec((1, r), lambda i: (0, 0))],
        out_specs=[pl.BlockSpec((ts, r), lambda i: (i, 0))] * 2,
        out_shape=[jax.ShapeDtypeStruct((s, r), F32)] * 2,
        compiler_params=_params("parallel"),
        name="rope_tables",
    )(pos_col, inv_freq2)


def _lnmod_mm_kernel(x_ref, sc_ref, sh_ref, *rest, mode, prenormed):
    if mode in ("glu", "swiglu"):
        wa_ref, wb_ref, o_ref, h_ref = rest
    else:
        wa_ref, o_ref, h_ref = rest
        wb_ref = None

    @pl.when(pl.program_id(1) == 0)
    def _():
        if prenormed:
            h_ref[...] = x_ref[...]
        else:
            h = _layer_norm_rows(x_ref[...]) * (1.0 + sc_ref[...]) + sh_ref[...]
            h_ref[...] = h.astype(BF16)

    h = h_ref[...]
    a = jnp.dot(h, wa_ref[...], preferred_element_type=F32)
    if mode == "plain":
        o = a
    elif mode == "sigmoid":
        o = _sigmoid(a)
    else:
        b = jnp.dot(h, wb_ref[...], preferred_element_type=F32)
        if mode == "glu":
            o = a * _sigmoid(b)
        else:
            o = a * _sigmoid(a) * b
    o_ref[...] = o.astype(o_ref.dtype)


def _lnmod_mm(x, sc, sh, wa, wb=None, *, mode, out_dtype, tm=512, tn=512, prenormed=False, name):
    s, d = x.shape
    n = wa.shape[1]
    dual = wb is not None
    w_spec = pl.BlockSpec((d, tn), lambda i, j: (0, j))
    vec_spec = pl.BlockSpec((1, d), lambda i, j: (0, 0))
    in_specs = [pl.BlockSpec((tm, d), lambda i, j: (i, 0)), vec_spec, vec_spec, w_spec]
    args = [x, sc, sh, wa]
    if dual:
        in_specs.append(w_spec)
        args.append(wb)
    return pl.pallas_call(
        functools.partial(_lnmod_mm_kernel, mode=mode, prenormed=prenormed),
        grid=(s // tm, n // tn),
        in_specs=in_specs,
        out_specs=pl.BlockSpec((tm, tn), lambda i, j: (i, j)),
        out_shape=jax.ShapeDtypeStruct((s, n), out_dtype),
        scratch_shapes=[pltpu.VMEM((tm, d), BF16)],
        compiler_params=_params("parallel", "arbitrary"),
        name=name,
    )(*args)


def _conv_pw_kernel(prev_ref, cur_ref, wdw_ref, bdw_ref, lng_ref, lnb_ref, wpw_ref, o_ref,
                    ext_ref, conv_ref, act_ref, *, tm, cw, rh):
    i = pl.program_id(0)

    @pl.when(pl.program_id(1) == 0)
    def _():
        d = cur_ref.shape[1]
        halo = prev_ref[...].astype(F32)
        ext_ref[0:CONV_HALO, :] = jnp.where(i > 0, halo, 0.0)
        ext_ref[CONV_HALO:CONV_HALO + tm, :] = cur_ref[...].astype(F32)
        first = CONV_HALO - (CONV_WIDTH - 1)

        def chan_chunk(ci, carry):
            cs = pl.ds(pl.multiple_of(ci * cw, cw), cw)
            for r0 in range(0, tm, rh):
                acc = jnp.zeros((rh, cw), F32)
                for k in range(CONV_WIDTH):
                    acc = acc + wdw_ref[k:k + 1, cs] * ext_ref[first + k + r0:first + k + r0 + rh, cs]
                conv_ref[r0:r0 + rh, cs] = acc
            return carry

        lax.fori_loop(0, d // cw, chan_chunk, 0)
        u = conv_ref[...] + bdw_ref[...]
        u = _layer_norm_rows(u) * lng_ref[...] + lnb_ref[...]
        act_ref[...] = (u * _sigmoid(u)).astype(BF16)

    o_ref[...] = jnp.dot(act_ref[...], wpw_ref[...], preferred_element_type=F32).astype(o_ref.dtype)


def _conv_pw(u, w_dw, b_dw, ln_g, ln_b, w_pw, *, tm=256, tn=512, cw=256, rh=64):
    s, d = u.shape
    n = w_pw.shape[1]
    ratio = tm // CONV_HALO
    vec = pl.BlockSpec((1, d), lambda i, j: (0, 0))
    return pl.pallas_call(
        functools.partial(_conv_pw_kernel, tm=tm, cw=cw, rh=rh),
        grid=(s // tm, n // tn),
        in_specs=[
            pl.BlockSpec((CONV_HALO, d), lambda i, j: (jnp.maximum(i * ratio - 1, 0), 0)),
            pl.BlockSpec((tm, d), lambda i, j: (i, 0)),
            pl.BlockSpec((CONV_WIDTH, d), lambda i, j: (0, 0)),
            vec, vec, vec,
            pl.BlockSpec((d, tn), lambda i, j: (0, j)),
        ],
        out_specs=pl.BlockSpec((tm, tn), lambda i, j: (i, j)),
        out_shape=jax.ShapeDtypeStruct((s, n), BF16),
        scratch_shapes=[
            pltpu.VMEM((CONV_HALO + tm, d), F32),
            pltpu.VMEM((tm, d), F32),
            pltpu.VMEM((tm, d), BF16),
        ],
        compiler_params=_params("parallel", "arbitrary"),
        name="conv_pw",
    )(u, u, w_dw, b_dw, ln_g, ln_b, w_pw)


def _rms_rows(x, g):
    return x * lax.rsqrt(jnp.mean(x * x, axis=-1, keepdims=True) + RMS_EPS) * g


def _qproj_kernel(cq_ref, g_ref, w_ref, wrot_ref, cos_ref, sin_ref, o_ref, cn_ref, *, scale):
    @pl.when(pl.program_id(1) == 0)
    def _():
        cn_ref[...] = _rms_rows(cq_ref[...], g_ref[...]).astype(BF16)

    cn = cn_ref[...]
    main = jnp.dot(cn, w_ref[0], preferred_element_type=F32)
    rot = jnp.dot(cn, wrot_ref[0], preferred_element_type=F32)
    rope = main[:, QK_NOPE_DIM:] * cos_ref[...] + rot * sin_ref[...]
    o_ref[0, :, 0:QK_NOPE_DIM] = (main[:, :QK_NOPE_DIM] * scale).astype(BF16)
    o_ref[0, :, QK_NOPE_DIM:QK_HEAD_DIM] = (rope * scale).astype(BF16)


def _qproj(lat, g_q, w_q, w_qrot, cos, sin, *, tm=512):
    s = lat.shape[0]
    h = w_q.shape[0]
    return pl.pallas_call(
        functools.partial(_qproj_kernel, scale=QK_HEAD_DIM ** -0.5),
        grid=(s // tm, h),
        in_specs=[
            pl.BlockSpec((tm, Q_LORA), lambda i, j: (i, 0)),
            pl.BlockSpec((1, Q_LORA), lambda i, j: (0, 0)),
            pl.BlockSpec((1, Q_LORA, QK_HEAD_DIM), lambda i, j: (j, 0, 0)),
            pl.BlockSpec((1, Q_LORA, QK_ROPE_DIM), lambda i, j: (j, 0, 0)),
            pl.BlockSpec((tm, QK_ROPE_DIM), lambda i, j: (i, 0)),
            pl.BlockSpec((tm, QK_ROPE_DIM), lambda i, j: (i, 0)),
        ],
        out_specs=pl.BlockSpec((1, tm, QK_HEAD_DIM), lambda i, j: (j, i, 0)),
        out_shape=jax.ShapeDtypeStruct((h, s, QK_HEAD_DIM), BF16),
        scratch_shapes=[pltpu.VMEM((tm, Q_LORA), BF16)],
        compiler_params=_params("parallel", "arbitrary"),
        name="q_proj",
    )(lat, g_q, w_q, w_qrot, cos, sin)


def _kvproj_kernel(ckv_ref, kr_ref, g_ref, w_ref, cos_ref, sin_ref, k_ref, v_ref, cn_ref, krope_ref):
    @pl.when(pl.program_id(1) == 0)
    def _():
        cn_ref[...] = _rms_rows(ckv_ref[...], g_ref[...]).astype(BF16)
        kr = kr_ref[:, 0:QK_ROPE_DIM]
        kr_rot = kr_ref[:, QK_ROPE_DIM:2 * QK_ROPE_DIM]
        krope_ref[...] = (kr * cos_ref[...] + kr_rot * sin_ref[...]).astype(BF16)

    kv = jnp.dot(cn_ref[...], w_ref[0], preferred_element_type=F32)
    k_ref[0, :, 0:QK_NOPE_DIM] = kv[:, :QK_NOPE_DIM].astype(BF16)
    k_ref[0, :, QK_NOPE_DIM:QK_HEAD_DIM] = krope_ref[...]
    v_ref[0] = kv[:, QK_NOPE_DIM:].astype(BF16)


def _kvproj(lat, g_kv, w_kv, cos, sin, *, tm=512):
    s = lat.shape[0]
    h = w_kv.shape[0]
    kr_block = (Q_LORA + KV_LORA) // (2 * QK_ROPE_DIM)
    return pl.pallas_call(
        _kvproj_kernel,
        grid=(s // tm, h),
        in_specs=[
            pl.BlockSpec((tm, KV_LORA), lambda i, j: (i, Q_LORA // KV_LORA)),
            pl.BlockSpec((tm, 2 * QK_ROPE_DIM), lambda i, j: (i, kr_block)),
            pl.BlockSpec((1, KV_LORA), lambda i, j: (0, 0)),
            pl.BlockSpec((1, KV_LORA, QK_NOPE_DIM + V_HEAD_DIM), lambda i, j: (j, 0, 0)),
            pl.BlockSpec((tm, QK_ROPE_DIM), lambda i, j: (i, 0)),
            pl.BlockSpec((tm, QK_ROPE_DIM), lambda i, j: (i, 0)),
        ],
        out_specs=[
            pl.BlockSpec((1, tm, QK_HEAD_DIM), lambda i, j: (j, i, 0)),
            pl.BlockSpec((1, tm, V_HEAD_DIM), lambda i, j: (j, i, 0)),
        ],
        out_shape=[
            jax.ShapeDtypeStruct((h, s, QK_HEAD_DIM), BF16),
            jax.ShapeDtypeStruct((h, s, V_HEAD_DIM), BF16),
        ],
        scratch_shapes=[pltpu.VMEM((tm, KV_LORA), BF16), pltpu.VMEM((tm, QK_ROPE_DIM), BF16)],
        compiler_params=_params("parallel", "arbitrary"),
        name="kv_proj",
    )(lat, lat, g_kv, w_kv, cos, sin)


def _flash_kernel(q_ref, k_ref, v_ref, o_ref, m_ref, l_ref, acc_ref, *, tq, tk):
    qi = pl.program_id(1)
    q = q_ref[0]
    m_ref[...] = jnp.full(m_ref.shape, -jnp.inf, F32)
    l_ref[...] = jnp.zeros(l_ref.shape, F32)
    acc_ref[...] = jnp.zeros(acc_ref.shape, F32)

    def step(j, mask):
        start = pl.multiple_of(j * tk, tk)
        k = k_ref[0, pl.ds(start, tk), :]
        v = v_ref[0, pl.ds(start, tk), :]
        s = lax.dot_general(q, k, (((1,), (1,)), ((), ())), preferred_element_type=F32)
        if mask is not None:
            s = jnp.where(mask, s, -jnp.inf)
        m_prev = m_ref[...]
        m_new = jnp.maximum(m_prev, jnp.max(s, axis=1, keepdims=True))
        alpha = jnp.exp(m_prev - m_new)
        p = jnp.exp(s - m_new)
        l_ref[...] = alpha * l_ref[...] + jnp.sum(p, axis=1, keepdims=True)
        acc_ref[...] = alpha * acc_ref[...] + jnp.dot(p.astype(BF16), v, preferred_element_type=F32)
        m_ref[...] = m_new

    n_full = qi * (tq // tk)

    def body(j, carry):
        step(j, None)
        return carry

    lax.fori_loop(0, n_full, body, 0)

    row_chunk = lax.broadcasted_iota(jnp.int32, (tq, tk), 0) // CHUNK
    col = lax.broadcasted_iota(jnp.int32, (tq, tk), 1)
    for dgl in range(tq // tk):
        mask = (col + dgl * tk) // CHUNK <= row_chunk
        step(n_full + dgl, mask)

    o_ref[...] = (acc_ref[...] / l_ref[...]).astype(o_ref.dtype)


def _flash(q, k, v, *, tq=256, tk=256):
    h, s, _ = q.shape
    return pl.pallas_call(
        functools.partial(_flash_kernel, tq=tq, tk=tk),
        grid=(h, s // tq),
        in_specs=[
            pl.BlockSpec((1, tq, QK_HEAD_DIM), lambda hh, i: (hh, i, 0)),
            pl.BlockSpec((1, s, QK_HEAD_DIM), lambda hh, i: (hh, 0, 0)),
            pl.BlockSpec((1, s, V_HEAD_DIM), lambda hh, i: (hh, 0, 0)),
        ],
        out_specs=pl.BlockSpec((tq, V_HEAD_DIM), lambda hh, i: (i, hh)),
        out_shape=jax.ShapeDtypeStruct((s, h * V_HEAD_DIM), BF16),
        scratch_shapes=[
            pltpu.VMEM((tq, 1), F32),
            pltpu.VMEM((tq, 1), F32),
            pltpu.VMEM((tq, V_HEAD_DIM), F32),
        ],
        compiler_params=_params("parallel", "arbitrary"),
        name="flash_attn",
    )(q, k, v)


def _wo_mix_kernel(o_ref, w_ref, gc_ref, gm_ref, yc_ref, out_ref):
    y_mla = jnp.dot(o_ref[...], w_ref[...], preferred_element_type=F32)
    mixed = gc_ref[...].astype(F32) * yc_ref[...].astype(F32) + gm_ref[...].astype(F32) * y_mla
    out_ref[...] = mixed.astype(out_ref.dtype)


def _wo_mix(o, w_o, gates, y_conv, *, tm=512, tn=512):
    s, d = o.shape
    n = w_o.shape[1]
    nb = n // tn
    return pl.pallas_call(
        _wo_mix_kernel,
        grid=(s // tm, nb),
        in_specs=[
            pl.BlockSpec((tm, d), lambda i, j: (i, 0)),
            pl.BlockSpec((d, tn), lambda i, j: (0, j)),
            pl.BlockSpec((tm, tn), lambda i, j: (i, j)),
            pl.BlockSpec((tm, tn), lambda i, j: (i, j + nb)),
            pl.BlockSpec((tm, tn), lambda i, j: (i, j)),
        ],
        out_specs=pl.BlockSpec((tm, tn), lambda i, j: (i, j)),
        out_shape=jax.ShapeDtypeStruct((s, n), BF16),
        compiler_params=_params("parallel", "parallel"),
        name="wo_mix",
    )(o, w_o, gates, gates, y_conv)


def _resid_ln(x, f, gate, lng, lnb):
    y = DEEPNORM_ALPHA * x + (1.0 + gate) * f
    return _layer_norm_rows(y) * lng + lnb


def _mm_resid_ln_kernel(a_ref, w_ref, x_ref, gate_ref, lng_ref, lnb_ref, o_ref, acc_ref):
    kk = pl.program_id(1)

    @pl.when(kk == 0)
    def _():
        acc_ref[...] = jnp.zeros(acc_ref.shape, F32)

    acc_ref[...] += jnp.dot(a_ref[...], w_ref[...], preferred_element_type=F32)

    @pl.when(kk == pl.num_programs(1) - 1)
    def _():
        o_ref[...] = _resid_ln(x_ref[...], acc_ref[...], gate_ref[...], lng_ref[...], lnb_ref[...])


def _mm_resid_ln(a, w, x, gate, lng, lnb, *, tm=512, tk=512, name):
    s, kdim = a.shape
    d = w.shape[1]
    vec = pl.BlockSpec((1, d), lambda i, k: (0, 0))
    return pl.pallas_call(
        _mm_resid_ln_kernel,
        grid=(s // tm, kdim // tk),
        in_specs=[
            pl.BlockSpec((tm, tk), lambda i, k: (i, k)),
            pl.BlockSpec((tk, d), lambda i, k: (k, 0)),
            pl.BlockSpec((tm, d), lambda i, k: (i, 0)),
            vec, vec, vec,
        ],
        out_specs=pl.BlockSpec((tm, d), lambda i, k: (i, 0)),
        out_shape=jax.ShapeDtypeStruct((s, d), F32),
        scratch_shapes=[pltpu.VMEM((tm, d), F32)],
        compiler_params=_params("parallel", "arbitrary"),
        name=name,
    )(a, w, x, gate, lng, lnb)


def _router_kernel(x_ref, sc_ref, sh_ref, wr_ref, h_ref, comb_ref):
    h = _layer_norm_rows(x_ref[...]) * (1.0 + sc_ref[...]) + sh_ref[...]
    h_ref[...] = h.astype(BF16)
    logits = jnp.dot(h, wr_ref[...], preferred_element_type=F32, precision=lax.Precision.HIGHEST)
    lane = lax.broadcasted_iota(jnp.int32, logits.shape, 1)
    neg = -jnp.inf
    l1 = jnp.where(lane < N_EXPERTS, logits, neg)
    v1 = jnp.max(l1, axis=1, keepdims=True)
    i1 = jnp.min(jnp.where(l1 == v1, lane, LANES), axis=1, keepdims=True)
    l2 = jnp.where(lane == i1, neg, l1)
    v2 = jnp.max(l2, axis=1, keepdims=True)
    i2 = jnp.min(jnp.where(l2 == v2, lane, LANES), axis=1, keepdims=True)
    e2 = jnp.exp(v2 - v1)
    w1 = 1.0 / (1.0 + e2)
    w2 = e2 / (1.0 + e2)
    comb_ref[...] = jnp.where(lane == i1, w1, 0.0) + jnp.where(lane == i2, w2, 0.0)


def _router(x, sc, sh, wr_pad, *, tm=256):
    s, d = x.shape
    vec = pl.BlockSpec((1, d), lambda i: (0, 0))
    return pl.pallas_call(
        _router_kernel,
        grid=(s // tm,),
        in_specs=[pl.BlockSpec((tm, d), lambda i: (i, 0)), vec, vec,
                  pl.BlockSpec((d, LANES), lambda i: (0, 0))],
        out_specs=[pl.BlockSpec((tm, d), lambda i: (i, 0)), pl.BlockSpec((tm, LANES), lambda i: (i, 0))],
        out_shape=[jax.ShapeDtypeStruct((s, d), BF16), jax.ShapeDtypeStruct((s, LANES), F32)],
        compiler_params=_params("parallel"),
        name="router",
    )(x, sc, sh, wr_pad)


def _mm_scale_acc_kernel(a_ref, w_ref, comb_ref, y_ref, o_ref, acc_ref):
    kk = pl.program_id(1)

    @pl.when(kk == 0)
    def _():
        acc_ref[...] = jnp.zeros(acc_ref.shape, F32)

    acc_ref[...] += jnp.dot(a_ref[...], w_ref[...], preferred_element_type=F32)

    @pl.when(kk == pl.num_programs(1) - 1)
    def _():
        o_ref[...] = y_ref[...] + comb_ref[...] * acc_ref[...]


def _mm_scale_acc(a, w, comb_col, y, *, tm=512, tk=512):
    s, kdim = a.shape
    d = w.shape[1]
    return pl.pallas_call(
        _mm_scale_acc_kernel,
        grid=(s // tm, kdim // tk),
        in_specs=[
            pl.BlockSpec((tm, tk), lambda i, k: (i, k)),
            pl.BlockSpec((tk, d), lambda i, k: (k, 0)),
            pl.BlockSpec((tm, 1), lambda i, k: (i, 0)),
            pl.BlockSpec((tm, d), lambda i, k: (i, 0)),
        ],
        out_specs=pl.BlockSpec((tm, d), lambda i, k: (i, 0)),
        out_shape=jax.ShapeDtypeStruct((s, d), F32),
        scratch_shapes=[pltpu.VMEM((tm, d), F32)],
        compiler_params=_params("parallel", "arbitrary"),
        name="expert_down",
    )(a, w, comb_col, y)


def _resid_ln_kernel(x_ref, f_ref, gate_ref, lng_ref, lnb_ref, o_ref):
    o_ref[...] = _resid_ln(x_ref[...], f_ref[...], gate_ref[...], lng_ref[...], lnb_ref[...])


def _resid_ln_call(x, f, gate, lng, lnb, *, tm=512):
    s, d = x.shape
    vec = pl.BlockSpec((1, d), lambda i: (0, 0))
    blk = pl.BlockSpec((tm, d), lambda i: (i, 0))
    return pl.pallas_call(
        _resid_ln_kernel,
        grid=(s // tm,),
        in_specs=[blk, blk, vec, vec, vec],
        out_specs=blk,
        out_shape=jax.ShapeDtypeStruct((s, d), F32),
        compiler_params=_params("parallel"),
        name="resid_ln",
    )(x, f, gate, lng, lnb)


def _rotate_half_cols(w):
    half = w.shape[-1] // 2
    return jnp.concatenate([-w[..., half:], w[..., :half]], axis=-1)


def kernel(x, c, positions, w_ada, b_ada, w_in, g_q, w_uq, g_kv, w_ukv, w_o, w_dw, b_dw, conv_ln_g, conv_ln_b, w_pw, w_out, ln1_g, ln1_b, ln2_g, ln2_b, w_ff_gate, w_ff_up, w_ff_down, w_router, w_e_gate, w_e_up, w_e_down):
    assert x.shape == (1, SEQ, D_MODEL) and c.shape == (1, D_MODEL)
    d = D_MODEL
    xs = x.reshape(SEQ, d)

    mod = _ada(c.reshape(d, 1), w_ada, b_ada).reshape(DEPTH, 6, 1, d)

    inv_freq = ROPE_THETA ** (-jnp.arange(0, QK_ROPE_DIM, 2, dtype=F32) / QK_ROPE_DIM)
    inv_freq2 = jnp.concatenate([inv_freq, inv_freq]).reshape(1, QK_ROPE_DIM)
    cos, sin = _rope_tables(positions.reshape(SEQ, 1), inv_freq2)

    row = lambda v: v.reshape(1, -1)

    for l in range(DEPTH):
        sh1, sc1, g1, sh2, sc2, g2 = (mod[l, t] for t in range(6))
        wl = w_in[l]
        o_kr = Q_LORA + KV_LORA
        o_val = o_kr + QK_ROPE_DIM
        w_kr = wl[:, o_kr:o_val]
        w_lat = jnp.concatenate([wl[:, :o_val], _rotate_half_cols(w_kr)], axis=1).astype(BF16)
        w_val = wl[:, o_val:o_val + d].astype(BF16)
        w_gate = wl[:, o_val + d:o_val + 2 * d].astype(BF16)
        w_brg = wl[:, o_val + 2 * d:].astype(BF16)

        lat = _lnmod_mm(xs, sc1, sh1, w_lat, mode="plain", out_dtype=F32, tn=LATENT_COLS // 3, name="in_latent")
        u = _lnmod_mm(xs, sc1, sh1, w_val, w_gate, mode="glu", out_dtype=BF16, name="in_glu")
        gates = _lnmod_mm(xs, sc1, sh1, w_brg, mode="sigmoid", out_dtype=BF16, name="in_gates")

        y_conv = _conv_pw(u, w_dw[l], row(b_dw[l]), row(conv_ln_g[l]), row(conv_ln_b[l]), w_pw[l].astype(BF16))

        wq = w_uq[l].reshape(Q_LORA, N_HEADS, QK_HEAD_DIM).transpose(1, 0, 2)
        wq_rot = _rotate_half_cols(wq[..., QK_NOPE_DIM:])
        wkv = w_ukv[l].reshape(KV_LORA, N_HEADS, QK_NOPE_DIM + V_HEAD_DIM).transpose(1, 0, 2)
        q = _qproj(lat, row(g_q[l]), wq.astype(BF16), wq_rot.astype(BF16), cos, sin)
        k, v = _kvproj(lat, row(g_kv[l]), wkv.astype(BF16), cos, sin)
        o = _flash(q, k, v)
        mixed = _wo_mix(o, w_o[l].astype(BF16), gates, y_conv)
        xs = _mm_resid_ln(mixed, w_out[l].astype(BF16), xs, g1, row(ln1_g[l]), row(ln1_b[l]), name="out_proj_ln")

        i = l // 2
        if l % 2 == 0:
            act = _lnmod_mm(xs, sc2, sh2, w_ff_gate[i].astype(BF16), w_ff_up[i].astype(BF16),
                            mode="swiglu", out_dtype=BF16, name="ffn_swiglu")
            xs = _mm_resid_ln(act, w_ff_down[i].astype(BF16), xs, g2, row(ln2_g[l]), row(ln2_b[l]), name="ffn_down_ln")
        else:
            wr_pad = jnp.zeros((d, LANES), F32).at[:, :N_EXPERTS].set(w_router[i])
            h2, comb = _router(xs, sc2, sh2, wr_pad)
            y = jnp.zeros((SEQ, d), F32)
            for e in range(N_EXPERTS):
                act = _lnmod_mm(h2, sc2, sh2, w_e_gate[i, e].astype(BF16), w_e_up[i, e].astype(BF16),
                                mode="swiglu", out_dtype=BF16, prenormed=True, name="expert_swiglu")
                y = _mm_scale_acc(act, w_e_down[i, e].astype(BF16), comb[:, e:e + 1], y)
            xs = _resid_ln_call(xs, y, g2, row(ln2_g[l]), row(ln2_b[l]))

    return xs.reshape(1, SEQ, D_MODEL)
```

```python
import functools

import jax
import jax.numpy as jnp
from jax import lax
from jax.experimental import pallas as pl
from jax.experimental.pallas import tpu as pltpu

F32 = jnp.float32
BF16 = jnp.bfloat16

D_MODEL = 2048
SEQ = 8192
DEPTH = 2
CHUNK = 64
N_HEADS = 16
QK_NOPE_DIM = 128
QK_ROPE_DIM = 64
V_HEAD_DIM = 128
QK_HEAD_DIM = QK_NOPE_DIM + QK_ROPE_DIM
Q_LORA = 512
KV_LORA = 512
ROPE_THETA = 10000.0
CONV_WIDTH = 31
D_FF_DENSE = 5632
N_EXPERTS = 8
D_FF_EXPERT = 7168
DEEPNORM_ALPHA = (2.0 * DEPTH) ** 0.25
LN_EPS = 1e-5
RMS_EPS = 1e-6
LATENT_COLS = Q_LORA + KV_LORA + 2 * QK_ROPE_DIM
CONV_HALO = 32
LANES = 128
SUBLANES = 8

VMEM_LIMIT = 48 * 1024 * 1024


def _params(*sem):
    return pltpu.CompilerParams(dimension_semantics=sem, vmem_limit_bytes=VMEM_LIMIT)


def _layer_norm_rows(x):
    mu = jnp.mean(x, axis=-1, keepdims=True)
    xc = x - mu
    var = jnp.mean(xc * xc, axis=-1, keepdims=True)
    return xc * lax.rsqrt(var + LN_EPS)


def _sigmoid(x):
    return 1.0 / (1.0 + jnp.exp(-x))


def _ada_kernel(c_ref, w_ref, b_ref, o_ref):
    c = c_ref[...]
    c_act = c * _sigmoid(c)
    o_ref[0] = jnp.sum(c_act * w_ref[0], axis=0, keepdims=True) + b_ref[0]


def _ada(c_col, w_ada, b_ada, tn=1024):
    depth, d, n = w_ada.shape
    return pl.pallas_call(
        _ada_kernel,
        grid=(depth, n // tn),
        in_specs=[
            pl.BlockSpec((d, 1), lambda l, j: (0, 0)),
            pl.BlockSpec((1, d, tn), lambda l, j: (l, 0, j)),
            pl.BlockSpec((1, 1, tn), lambda l, j: (l, 0, j)),
        ],
        out_specs=pl.BlockSpec((1, 1, tn), lambda l, j: (l, 0, j)),
        out_shape=jax.ShapeDtypeStruct((depth, 1, n), F32),
        compiler_params=_params("parallel", "parallel"),
        name="ada_mod",
    )(c_col, w_ada, b_ada.reshape(depth, 1, n))


def _rope_kernel(pos_col_ref, pos_row_ref, inv_row_ref, inv_col_ref, cos_ref, sin_ref, cos_t_ref, sin_t_ref):
    ang = pos_col_ref[...].astype(F32) * inv_row_ref[...]
    cos_ref[...] = jnp.cos(ang)
    sin_ref[...] = jnp.sin(ang)
    ang_t = inv_col_ref[...] * pos_row_ref[...].astype(F32)
    cos_t_ref[...] = jnp.cos(ang_t)
    sin_t_ref[...] = jnp.sin(ang_t)


def _rope_tables(positions, inv_freq2, ts=512):
    s = positions.shape[0]
    r = inv_freq2.shape[0]
    return pl.pallas_call(
        _rope_kernel,
        grid=(s // ts,),
        in_specs=[
            pl.BlockSpec((ts, 1), lambda i: (i, 0)),
            pl.BlockSpec((1, ts), lambda i: (0, i)),
            pl.BlockSpec((1, r), lambda i: (0, 0)),
            pl.BlockSpec((r, 1), lambda i: (0, 0)),
        ],
        out_specs=[pl.BlockSpec((ts, r), lambda i: (i, 0))] * 2 + [pl.BlockSpec((r, ts), lambda i: (0, i))] * 2,
        out_shape=[jax.ShapeDtypeStruct((s, r), F32)] * 2 + [jax.ShapeDtypeStruct((r, s), F32)] * 2,
        compiler_params=_params("parallel"),
        name="rope_tables",
    )(positions.reshape(s, 1), positions.reshape(1, s), inv_freq2.reshape(1, r), inv_freq2.reshape(r, 1))


def _lnmod_mm_kernel(x_ref, sc_ref, sh_ref, *rest, mode, prenormed):
    if mode in ("glu", "swiglu"):
        wa_ref, wb_ref, o_ref, h_ref = rest
    else:
        wa_ref, o_ref, h_ref = rest
        wb_ref = None

    @pl.when(pl.program_id(1) == 0)
    def _():
        if prenormed:
            h_ref[...] = x_ref[...]
        else:
            h = _layer_norm_rows(x_ref[...]) * (1.0 + sc_ref[...]) + sh_ref[...]
            h_ref[...] = h.astype(BF16)

    h = h_ref[...]
    a = jnp.dot(h, wa_ref[...], preferred_element_type=F32)
    if mode == "plain":
        o = a
    elif mode == "sigmoid":
        o = _sigmoid(a)
    else:
        b = jnp.dot(h, wb_ref[...], preferred_element_type=F32)
        if mode == "glu":
            o = a * _sigmoid(b)
        else:
            o = a * _sigmoid(a) * b
    o_ref[...] = o.astype(o_ref.dtype)


def _lnmod_mm(x, sc, sh, wa, wb=None, *, mode, out_dtype, tm=512, tn=512, prenormed=False, name):
    s, d = x.shape
    n = wa.shape[1]
    dual = wb is not None
    w_spec = pl.BlockSpec((d, tn), lambda i, j: (0, j))
    vec_spec = pl.BlockSpec((1, d), lambda i, j: (0, 0))
    in_specs = [pl.BlockSpec((tm, d), lambda i, j: (i, 0)), vec_spec, vec_spec, w_spec]
    args = [x, sc, sh, wa]
    if dual:
        in_specs.append(w_spec)
        args.append(wb)
    return pl.pallas_call(
        functools.partial(_lnmod_mm_kernel, mode=mode, prenormed=prenormed),
        grid=(s // tm, n // tn),
        in_specs=in_specs,
        out_specs=pl.BlockSpec((tm, tn), lambda i, j: (i, j)),
        out_shape=jax.ShapeDtypeStruct((s, n), out_dtype),
        scratch_shapes=[pltpu.VMEM((tm, d), BF16)],
        compiler_params=_params("parallel", "arbitrary"),
        name=name,
    )(*args)


def _conv_pw_kernel(prev_ref, cur_ref, wdw_ref, bdw_ref, lng_ref, lnb_ref, wpw_ref, o_ref,
                    ext_ref, conv_ref, act_ref, *, tm, cw, rh):
    i = pl.program_id(0)

    @pl.when(pl.program_id(1) == 0)
    def _():
        d = cur_ref.shape[1]
        halo = prev_ref[...].astype(F32)
        ext_ref[0:CONV_HALO, :] = jnp.where(i > 0, halo, 0.0)
        ext_ref[CONV_HALO:CONV_HALO + tm, :] = cur_ref[...].astype(F32)
        first = CONV_HALO - (CONV_WIDTH - 1)

        def chan_chunk(ci, carry):
            cs = pl.ds(pl.multiple_of(ci * cw, cw), cw)
            for r0 in range(0, tm, rh):
                partial = []
                for res in range(SUBLANES):
                    taps = [k for k in range(CONV_WIDTH) if (first + k) % SUBLANES == res]
                    lo = min(first + k for k in taps) - res
                    hi = max(first + k for k in taps) - res
                    shifted = ext_ref[r0 + lo + res:r0 + hi + res + rh, cs]
                    acc = None
                    for k in taps:
                        off = first + k - res - lo
                        prod = wdw_ref[k:k + 1, cs] * shifted[off:off + rh]
                        acc = prod if acc is None else acc + prod
                    partial.append(acc)
                while len(partial) > 1:
                    partial = [a + b for a, b in zip(partial[0::2], partial[1::2])]
                conv_ref[r0:r0 + rh, cs] = partial[0]
            return carry

        lax.fori_loop(0, d // cw, chan_chunk, 0)
        u = conv_ref[...] + bdw_ref[...]
        u = _layer_norm_rows(u) * lng_ref[...] + lnb_ref[...]
        act_ref[...] = (u * _sigmoid(u)).astype(BF16)

    o_ref[...] = jnp.dot(act_ref[...], wpw_ref[...], preferred_element_type=F32).astype(o_ref.dtype)


def _conv_pw(u, w_dw, b_dw, ln_g, ln_b, w_pw, *, tm=256, tn=512, cw=256, rh=64):
    s, d = u.shape
    n = w_pw.shape[1]
    ratio = tm // CONV_HALO
    vec = pl.BlockSpec((1, d), lambda i, j: (0, 0))
    return pl.pallas_call(
        functools.partial(_conv_pw_kernel, tm=tm, cw=cw, rh=rh),
        grid=(s // tm, n // tn),
        in_specs=[
            pl.BlockSpec((CONV_HALO, d), lambda i, j: (jnp.maximum(i * ratio - 1, 0), 0)),
            pl.BlockSpec((tm, d), lambda i, j: (i, 0)),
            pl.BlockSpec((CONV_WIDTH, d), lambda i, j: (0, 0)),
            vec, vec, vec,
            pl.BlockSpec((d, tn), lambda i, j: (0, j)),
        ],
        out_specs=pl.BlockSpec((tm, tn), lambda i, j: (i, j)),
        out_shape=jax.ShapeDtypeStruct((s, n), BF16),
        scratch_shapes=[
            pltpu.VMEM((CONV_HALO + tm, d), F32),
            pltpu.VMEM((tm, d), F32),
            pltpu.VMEM((tm, d), BF16),
        ],
        compiler_params=_params("parallel", "arbitrary"),
        name="conv_pw",
    )(u, u, w_dw, b_dw, ln_g, ln_b, w_pw)


def _rms_rows(x, g):
    return x * lax.rsqrt(jnp.mean(x * x, axis=-1, keepdims=True) + RMS_EPS) * g


_NT = (((1,), (1,)), ((), ()))


def _qproj_kernel(cq_ref, g_ref, wt_ref, wrot_t_ref, cos_t_ref, sin_t_ref, o_ref, *, scale):
    cn = _rms_rows(cq_ref[...], g_ref[...]).astype(BF16)
    main_t = lax.dot_general(wt_ref[...], cn, _NT, preferred_element_type=F32)
    rot_t = lax.dot_general(wrot_t_ref[...], cn, _NT, preferred_element_type=F32)
    cos_t = cos_t_ref[...]
    sin_t = sin_t_ref[...]
    n_nope = N_HEADS * QK_NOPE_DIM
    for h in range(N_HEADS):
        nope = main_t[h * QK_NOPE_DIM:(h + 1) * QK_NOPE_DIM]
        r0 = h * QK_ROPE_DIM
        rope = main_t[n_nope + r0:n_nope + r0 + QK_ROPE_DIM] * cos_t + rot_t[r0:r0 + QK_ROPE_DIM] * sin_t
        o_ref[h, 0:QK_NOPE_DIM, :] = (nope * scale).astype(BF16)
        o_ref[h, QK_NOPE_DIM:QK_HEAD_DIM, :] = (rope * scale).astype(BF16)


def _qproj(lat, g_q, wq_t, wq_rot_t, cos_t, sin_t, *, tm=256):
    s = lat.shape[0]
    scale = QK_HEAD_DIM ** -0.5 * 1.4426950408889634
    return pl.pallas_call(
        functools.partial(_qproj_kernel, scale=scale),
        grid=(s // tm,),
        in_specs=[
            pl.BlockSpec((tm, Q_LORA), lambda i: (i, 0)),
            pl.BlockSpec((1, Q_LORA), lambda i: (0, 0)),
            pl.BlockSpec(wq_t.shape, lambda i: (0, 0)),
            pl.BlockSpec(wq_rot_t.shape, lambda i: (0, 0)),
            pl.BlockSpec((QK_ROPE_DIM, tm), lambda i: (0, i)),
            pl.BlockSpec((QK_ROPE_DIM, tm), lambda i: (0, i)),
        ],
        out_specs=pl.BlockSpec((N_HEADS, QK_HEAD_DIM, tm), lambda i: (0, 0, i)),
        out_shape=jax.ShapeDtypeStruct((N_HEADS, QK_HEAD_DIM, s), BF16),
        compiler_params=_params("parallel"),
        name="q_proj",
    )(lat, g_q, wq_t, wq_rot_t, cos_t, sin_t)


def _kvproj_kernel(ckv_ref, kr_ref, g_ref, wk_ref, wv_t_ref, cos_ref, sin_ref, k_ref, vt_ref):
    cn = _rms_rows(ckv_ref[...], g_ref[...]).astype(BF16)
    kr = kr_ref[:, 0:QK_ROPE_DIM]
    kr_rot = kr_ref[:, QK_ROPE_DIM:2 * QK_ROPE_DIM]
    krope = (kr * cos_ref[...] + kr_rot * sin_ref[...]).astype(BF16)
    k_all = jnp.dot(cn, wk_ref[...], preferred_element_type=F32)
    v_t = lax.dot_general(wv_t_ref[...], cn, _NT, preferred_element_type=F32)
    for h in range(N_HEADS):
        k_ref[h, :, 0:QK_NOPE_DIM] = k_all[:, h * QK_NOPE_DIM:(h + 1) * QK_NOPE_DIM].astype(BF16)
        k_ref[h, :, QK_NOPE_DIM:QK_HEAD_DIM] = krope
        vt_ref[h] = v_t[h * V_HEAD_DIM:(h + 1) * V_HEAD_DIM].astype(BF16)


def _kvproj(lat, g_kv, wk, wv_t, cos, sin, *, tm=256):
    s = lat.shape[0]
    kr_block = (Q_LORA + KV_LORA) // (2 * QK_ROPE_DIM)
    return pl.pallas_call(
        _kvproj_kernel,
        grid=(s // tm,),
        in_specs=[
            pl.BlockSpec((tm, KV_LORA), lambda i: (i, Q_LORA // KV_LORA)),
            pl.BlockSpec((tm, 2 * QK_ROPE_DIM), lambda i: (i, kr_block)),
            pl.BlockSpec((1, KV_LORA), lambda i: (0, 0)),
            pl.BlockSpec(wk.shape, lambda i: (0, 0)),
            pl.BlockSpec(wv_t.shape, lambda i: (0, 0)),
            pl.BlockSpec((tm, QK_ROPE_DIM), lambda i: (i, 0)),
            pl.BlockSpec((tm, QK_ROPE_DIM), lambda i: (i, 0)),
        ],
        out_specs=[
            pl.BlockSpec((N_HEADS, tm, QK_HEAD_DIM), lambda i: (0, i, 0)),
            pl.BlockSpec((N_HEADS, V_HEAD_DIM, tm), lambda i: (0, 0, i)),
        ],
        out_shape=[
            jax.ShapeDtypeStruct((N_HEADS, s, QK_HEAD_DIM), BF16),
            jax.ShapeDtypeStruct((N_HEADS, V_HEAD_DIM, s), BF16),
        ],
        compiler_params=_params("parallel"),
        name="kv_proj",
    )(lat, lat, g_kv, wk, wv_t, cos, sin)


def _flash_kernel(qt_ref, k_ref, vt_ref, o_ref, sa_ref, sb_ref, m_ref, l_ref, acc_ref, *, t):
    qi = pl.program_id(1)
    qt = qt_ref[0]

    def scores(j, s_ref):
        start = pl.multiple_of(j * t, t)
        s_ref[...] = jnp.dot(k_ref[0, pl.ds(start, t), :], qt, preferred_element_type=F32)

    def consume(j, s_ref, mask):
        s = s_ref[...]
        if mask is not None:
            s = jnp.where(mask, s, -jnp.inf)
        m_prev = m_ref[...]
        m_new = jnp.maximum(m_prev, jnp.max(s, axis=0, keepdims=True))
        alpha = jnp.exp2(m_prev - m_new)
        p = jnp.exp2(s - m_new)
        l_ref[...] = alpha * l_ref[...] + jnp.sum(p, axis=0, keepdims=True)
        start = pl.multiple_of(j * t, t)
        pv = jnp.dot(vt_ref[0, :, pl.ds(start, t)], p.astype(BF16), preferred_element_type=F32)
        acc_ref[...] = alpha * acc_ref[...] + pv
        m_ref[...] = m_new

    m_ref[...] = jnp.full(m_ref.shape, -jnp.inf, F32)
    l_ref[...] = jnp.zeros(l_ref.shape, F32)
    acc_ref[...] = jnp.zeros(acc_ref.shape, F32)

    scores(0, sa_ref)

    def pair(tt, carry):
        j = 2 * tt
        scores(j + 1, sb_ref)
        consume(j, sa_ref, None)
        scores(j + 2, sa_ref)
        consume(j + 1, sb_ref, None)
        return carry

    lax.fori_loop(0, qi // 2, pair, 0)

    key_chunk = lax.broadcasted_iota(jnp.int32, (t, t), 0) // CHUNK
    query_chunk = lax.broadcasted_iota(jnp.int32, (t, t), 1) // CHUNK
    diag_mask = key_chunk <= query_chunk

    @pl.when(qi % 2 == 0)
    def _():
        consume(qi, sa_ref, diag_mask)

    @pl.when(qi % 2 == 1)
    def _():
        scores(qi, sb_ref)
        consume(qi - 1, sa_ref, None)
        consume(qi, sb_ref, diag_mask)

    out_t = acc_ref[...] / l_ref[...]
    o_ref[...] = out_t.T.astype(o_ref.dtype)


def _flash(q_t, k, v_t, *, t=512):
    h, _, s = q_t.shape
    return pl.pallas_call(
        functools.partial(_flash_kernel, t=t),
        grid=(h, s // t),
        in_specs=[
            pl.BlockSpec((1, QK_HEAD_DIM, t), lambda hh, i: (hh, 0, i)),
            pl.BlockSpec((1, s, QK_HEAD_DIM), lambda hh, i: (hh, 0, 0)),
            pl.BlockSpec((1, V_HEAD_DIM, s), lambda hh, i: (hh, 0, 0)),
        ],
        out_specs=pl.BlockSpec((t, V_HEAD_DIM), lambda hh, i: (i, hh)),
        out_shape=jax.ShapeDtypeStruct((s, h * V_HEAD_DIM), BF16),
        scratch_shapes=[
            pltpu.VMEM((t, t), F32),
            pltpu.VMEM((t, t), F32),
            pltpu.VMEM((1, t), F32),
            pltpu.VMEM((1, t), F32),
            pltpu.VMEM((V_HEAD_DIM, t), F32),
        ],
        compiler_params=_params("parallel", "arbitrary"),
        name="flash_attn",
    )(q_t, k, v_t)


def _wo_mix_kernel(o_ref, w_ref, gc_ref, gm_ref, yc_ref, out_ref):
    y_mla = jnp.dot(o_ref[...], w_ref[...], preferred_element_type=F32)
    mixed = gc_ref[...].astype(F32) * yc_ref[...].astype(F32) + gm_ref[...].astype(F32) * y_mla
    out_ref[...] = mixed.astype(out_ref.dtype)


def _wo_mix(o, w_o, gates, y_conv, *, tm=512, tn=512):
    s, d = o.shape
    n = w_o.shape[1]
    nb = n // tn
    return pl.pallas_call(
        _wo_mix_kernel,
        grid=(s // tm, nb),
        in_specs=[
            pl.BlockSpec((tm, d), lambda i, j: (i, 0)),
            pl.BlockSpec((d, tn), lambda i, j: (0, j)),
            pl.BlockSpec((tm, tn), lambda i, j: (i, j)),
            pl.BlockSpec((tm, tn), lambda i, j: (i, j + nb)),
            pl.BlockSpec((tm, tn), lambda i, j: (i, j)),
        ],
        out_specs=pl.BlockSpec((tm, tn), lambda i, j: (i, j)),
        out_shape=jax.ShapeDtypeStruct((s, n), BF16),
        compiler_params=_params("parallel", "parallel"),
        name="wo_mix",
    )(o, w_o, gates, gates, y_conv)


def _resid_ln(x, f, gate, lng, lnb):
    y = DEEPNORM_ALPHA * x + (1.0 + gate) * f
    return _layer_norm_rows(y) * lng + lnb


def _mm_resid_ln_kernel(a_ref, w_ref, x_ref, gate_ref, lng_ref, lnb_ref, o_ref, acc_ref):
    kk = pl.program_id(1)

    @pl.when(kk == 0)
    def _():
        acc_ref[...] = jnp.zeros(acc_ref.shape, F32)

    acc_ref[...] += jnp.dot(a_ref[...], w_ref[...], preferred_element_type=F32)

    @pl.when(kk == pl.num_programs(1) - 1)
    def _():
        o_ref[...] = _resid_ln(x_ref[...], acc_ref[...], gate_ref[...], lng_ref[...], lnb_ref[...])


def _mm_resid_ln(a, w, x, gate, lng, lnb, *, tm=512, tk=512, name):
    s, kdim = a.shape
    d = w.shape[1]
    vec = pl.BlockSpec((1, d), lambda i, k: (0, 0))
    return pl.pallas_call(
        _mm_resid_ln_kernel,
        grid=(s // tm, kdim // tk),
        in_specs=[
            pl.BlockSpec((tm, tk), lambda i, k: (i, k)),
            pl.BlockSpec((tk, d), lambda i, k: (k, 0)),
            pl.BlockSpec((tm, d), lambda i, k: (i, 0)),
            vec, vec, vec,
        ],
        out_specs=pl.BlockSpec((tm, d), lambda i, k: (i, 0)),
        out_shape=jax.ShapeDtypeStruct((s, d), F32),
        scratch_shapes=[pltpu.VMEM((tm, d), F32)],
        compiler_params=_params("parallel", "arbitrary"),
        name=name,
    )(a, w, x, gate, lng, lnb)


def _router_kernel(x_ref, sc_ref, sh_ref, wr_ref, h_ref, comb_ref):
    h = _layer_norm_rows(x_ref[...]) * (1.0 + sc_ref[...]) + sh_ref[...]
    h_ref[...] = h.astype(BF16)
    logits = jnp.dot(h, wr_ref[...], preferred_element_type=F32, precision=lax.Precision.HIGHEST)
    lane = lax.broadcasted_iota(jnp.int32, logits.shape, 1)
    neg = -jnp.inf
    l1 = jnp.where(lane < N_EXPERTS, logits, neg)
    v1 = jnp.max(l1, axis=1, keepdims=True)
    i1 = jnp.min(jnp.where(l1 == v1, lane, LANES), axis=1, keepdims=True)
    l2 = jnp.where(lane == i1, neg, l1)
    v2 = jnp.max(l2, axis=1, keepdims=True)
    i2 = jnp.min(jnp.where(l2 == v2, lane, LANES), axis=1, keepdims=True)
    e2 = jnp.exp(v2 - v1)
    w1 = 1.0 / (1.0 + e2)
    w2 = e2 / (1.0 + e2)
    comb_ref[...] = jnp.where(lane == i1, w1, 0.0) + jnp.where(lane == i2, w2, 0.0)


def _router(x, sc, sh, wr_pad, *, tm=256):
    s, d = x.shape
    vec = pl.BlockSpec((1, d), lambda i: (0, 0))
    return pl.pallas_call(
        _router_kernel,
        grid=(s // tm,),
        in_specs=[pl.BlockSpec((tm, d), lambda i: (i, 0)), vec, vec,
                  pl.BlockSpec((d, LANES), lambda i: (0, 0))],
        out_specs=[pl.BlockSpec((tm, d), lambda i: (i, 0)), pl.BlockSpec((tm, LANES), lambda i: (i, 0))],
        out_shape=[jax.ShapeDtypeStruct((s, d), BF16), jax.ShapeDtypeStruct((s, LANES), F32)],
        compiler_params=_params("parallel"),
        name="router",
    )(x, sc, sh, wr_pad)


def _mm_scale_acc_kernel(a_ref, w_ref, comb_ref, y_ref, o_ref, acc_ref):
    kk = pl.program_id(1)

    @pl.when(kk == 0)
    def _():
        acc_ref[...] = jnp.zeros(acc_ref.shape, F32)

    acc_ref[...] += jnp.dot(a_ref[...], w_ref[...], preferred_element_type=F32)

    @pl.when(kk == pl.num_programs(1) - 1)
    def _():
        o_ref[...] = y_ref[...] + comb_ref[...] * acc_ref[...]


def _mm_scale_acc(a, w, comb_col, y, *, tm=512, tk=512):
    s, kdim = a.shape
    d = w.shape[1]
    return pl.pallas_call(
        _mm_scale_acc_kernel,
        grid=(s // tm, kdim // tk),
        in_specs=[
            pl.BlockSpec((tm, tk), lambda i, k: (i, k)),
            pl.BlockSpec((tk, d), lambda i, k: (k, 0)),
            pl.BlockSpec((tm, 1), lambda i, k: (i, 0)),
            pl.BlockSpec((tm, d), lambda i, k: (i, 0)),
        ],
        out_specs=pl.BlockSpec((tm, d), lambda i, k: (i, 0)),
        out_shape=jax.ShapeDtypeStruct((s, d), F32),
        scratch_shapes=[pltpu.VMEM((tm, d), F32)],
        compiler_params=_params("parallel", "arbitrary"),
        name="expert_down",
    )(a, w, comb_col, y)


def _resid_ln_kernel(x_ref, f_ref, gate_ref, lng_ref, lnb_ref, o_ref):
    o_ref[...] = _resid_ln(x_ref[...], f_ref[...], gate_ref[...], lng_ref[...], lnb_ref[...])


def _resid_ln_call(x, f, gate, lng, lnb, *, tm=512):
    s, d = x.shape
    vec = pl.BlockSpec((1, d), lambda i: (0, 0))
    blk = pl.BlockSpec((tm, d), lambda i: (i, 0))
    return pl.pallas_call(
        _resid_ln_kernel,
        grid=(s // tm,),
        in_specs=[blk, blk, vec, vec, vec],
        out_specs=blk,
        out_shape=jax.ShapeDtypeStruct((s, d), F32),
        compiler_params=_params("parallel"),
        name="resid_ln",
    )(x, f, gate, lng, lnb)


def _rotate_half_cols(w):
    half = w.shape[-1] // 2
    return jnp.concatenate([-w[..., half:], w[..., :half]], axis=-1)


def kernel(x, c, positions, w_ada, b_ada, w_in, g_q, w_uq, g_kv, w_ukv, w_o, w_dw, b_dw, conv_ln_g, conv_ln_b, w_pw, w_out, ln1_g, ln1_b, ln2_g, ln2_b, w_ff_gate, w_ff_up, w_ff_down, w_router, w_e_gate, w_e_up, w_e_down):
    assert x.shape == (1, SEQ, D_MODEL) and c.shape == (1, D_MODEL)
    d = D_MODEL
    xs = x.reshape(SEQ, d)

    mod = _ada(c.reshape(d, 1), w_ada, b_ada).reshape(DEPTH, 6, 1, d)

    inv_freq = ROPE_THETA ** (-jnp.arange(0, QK_ROPE_DIM, 2, dtype=F32) / QK_ROPE_DIM)
    inv_freq2 = jnp.concatenate([inv_freq, inv_freq])
    cos, sin, cos_t, sin_t = _rope_tables(positions.reshape(SEQ), inv_freq2)

    row = lambda v: v.reshape(1, -1)

    for l in range(DEPTH):
        sh1, sc1, g1, sh2, sc2, g2 = (mod[l, t] for t in range(6))
        wl = w_in[l]
        o_kr = Q_LORA + KV_LORA
        o_val = o_kr + QK_ROPE_DIM
        w_kr = wl[:, o_kr:o_val]
        w_lat = jnp.concatenate([wl[:, :o_val], _rotate_half_cols(w_kr)], axis=1).astype(BF16)
        w_val = wl[:, o_val:o_val + d].astype(BF16)
        w_gate = wl[:, o_val + d:o_val + 2 * d].astype(BF16)
        w_brg = wl[:, o_val + 2 * d:].astype(BF16)

        lat = _lnmod_mm(xs, sc1, sh1, w_lat, mode="plain", out_dtype=F32, tn=LATENT_COLS // 3, name="in_latent")
        u = _lnmod_mm(xs, sc1, sh1, w_val, w_gate, mode="glu", out_dtype=BF16, name="in_glu")
        gates = _lnmod_mm(xs, sc1, sh1, w_brg, mode="sigmoid", out_dtype=BF16, name="in_gates")

        y_conv = _conv_pw(u, w_dw[l], row(b_dw[l]), row(conv_ln_g[l]), row(conv_ln_b[l]), w_pw[l].astype(BF16))

        wq = w_uq[l].reshape(Q_LORA, N_HEADS, QK_HEAD_DIM)
        wq_rope = wq[..., QK_NOPE_DIM:]
        wq_t = jnp.concatenate([wq[..., :QK_NOPE_DIM].reshape(Q_LORA, -1), wq_rope.reshape(Q_LORA, -1)], axis=1).T
        wq_rot_t = _rotate_half_cols(wq_rope).reshape(Q_LORA, -1).T
        wkv = w_ukv[l].reshape(KV_LORA, N_HEADS, QK_NOPE_DIM + V_HEAD_DIM)
        wk = wkv[..., :QK_NOPE_DIM].reshape(KV_LORA, -1)
        wv_t = wkv[..., QK_NOPE_DIM:].reshape(KV_LORA, -1).T
        q_t = _qproj(lat, row(g_q[l]), wq_t.astype(BF16), wq_rot_t.astype(BF16), cos_t, sin_t)
        k, v_t = _kvproj(lat, row(g_kv[l]), wk.astype(BF16), wv_t.astype(BF16), cos, sin)
        o = _flash(q_t, k, v_t)
        mixed = _wo_mix(o, w_o[l].astype(BF16), gates, y_conv)
        xs = _mm_resid_ln(mixed, w_out[l].astype(BF16), xs, g1, row(ln1_g[l]), row(ln1_b[l]), name="out_proj_ln")

        i = l // 2
        if l % 2 == 0:
            act = _lnmod_mm(xs, sc2, sh2, w_ff_gate[i].astype(BF16), w_ff_up[i].astype(BF16),
                            mode="swiglu", out_dtype=BF16, name="ffn_swiglu")
            xs = _mm_resid_ln(act, w_ff_down[i].astype(BF16), xs, g2, row(ln2_g[l]), row(ln2_b[l]), name="ffn_down_ln")
        else:
            wr_pad = jnp.zeros((d, LANES), F32).at[:, :N_EXPERTS].set(w_router[i])
            h2, comb = _router(xs, sc2, sh2, wr_pad)
            y = jnp.zeros((SEQ, d), F32)
            for e in range(N_EXPERTS):
                act = _lnmod_mm(h2, sc2, sh2, w_e_gate[i, e].astype(BF16), w_e_up[i, e].astype(BF16),
                                mode="swiglu", out_dtype=BF16, prenormed=True, name="expert_swiglu")
                y = _mm_scale_acc(act, w_e_down[i, e].astype(BF16), comb[:, e:e + 1], y)
            xs = _resid_ln_call(xs, y, g2, row(ln2_g[l]), row(ln2_b[l]))

    return xs.reshape(1, SEQ, D_MODEL)
```

```python
import functools

import jax
import jax.numpy as jnp
from jax import lax
from jax.experimental import pallas as pl
from jax.experimental.pallas import tpu as pltpu

F32 = jnp.float32
BF16 = jnp.bfloat16

D_MODEL = 2048
SEQ = 8192
DEPTH = 2
CHUNK = 64
N_HEADS = 16
QK_NOPE_DIM = 128
QK_ROPE_DIM = 64
V_HEAD_DIM = 128
QK_HEAD_DIM = QK_NOPE_DIM + QK_ROPE_DIM
Q_LORA = 512
KV_LORA = 512
ROPE_THETA = 10000.0
CONV_WIDTH = 31
D_FF_DENSE = 5632
N_EXPERTS = 8
D_FF_EXPERT = 7168
DEEPNORM_ALPHA = (2.0 * DEPTH) ** 0.25
LN_EPS = 1e-5
RMS_EPS = 1e-6
LATENT_COLS = Q_LORA + KV_LORA + 2 * QK_ROPE_DIM
CONV_HALO = 32
LANES = 128
SUBLANES = 8
EXPERT_ROW_TILE = 512

VMEM_LIMIT = 48 * 1024 * 1024


def _params(*sem):
    return pltpu.CompilerParams(dimension_semantics=sem, vmem_limit_bytes=VMEM_LIMIT)


def _layer_norm_rows(x):
    mu = jnp.mean(x, axis=-1, keepdims=True)
    xc = x - mu
    var = jnp.mean(xc * xc, axis=-1, keepdims=True)
    return xc * lax.rsqrt(var + LN_EPS)


def _sigmoid(x):
    return 1.0 / (1.0 + jnp.exp(-x))


def _ada_kernel(c_ref, w_ref, b_ref, o_ref):
    c = c_ref[...]
    c_act = c * _sigmoid(c)
    o_ref[0] = jnp.sum(c_act * w_ref[0], axis=0, keepdims=True) + b_ref[0]


def _ada(c_col, w_ada, b_ada, tn=1024):
    depth, d, n = w_ada.shape
    return pl.pallas_call(
        _ada_kernel,
        grid=(depth, n // tn),
        in_specs=[
            pl.BlockSpec((d, 1), lambda l, j: (0, 0)),
            pl.BlockSpec((1, d, tn), lambda l, j: (l, 0, j)),
            pl.BlockSpec((1, 1, tn), lambda l, j: (l, 0, j)),
        ],
        out_specs=pl.BlockSpec((1, 1, tn), lambda l, j: (l, 0, j)),
        out_shape=jax.ShapeDtypeStruct((depth, 1, n), F32),
        compiler_params=_params("parallel", "parallel"),
        name="ada_mod",
    )(c_col, w_ada, b_ada.reshape(depth, 1, n))


def _rope_kernel(pos_col_ref, pos_row_ref, inv_row_ref, inv_col_ref, cos_ref, sin_ref, cos_t_ref, sin_t_ref):
    ang = pos_col_ref[...].astype(F32) * inv_row_ref[...]
    cos_ref[...] = jnp.cos(ang)
    sin_ref[...] = jnp.sin(ang)
    ang_t = inv_col_ref[...] * pos_row_ref[...].astype(F32)
    cos_t_ref[...] = jnp.cos(ang_t)
    sin_t_ref[...] = jnp.sin(ang_t)


def _rope_tables(positions, inv_freq2, ts=512):
    s = positions.shape[0]
    r = inv_freq2.shape[0]
    return pl.pallas_call(
        _rope_kernel,
        grid=(s // ts,),
        in_specs=[
            pl.BlockSpec((ts, 1), lambda i: (i, 0)),
            pl.BlockSpec((1, ts), lambda i: (0, i)),
            pl.BlockSpec((1, r), lambda i: (0, 0)),
            pl.BlockSpec((r, 1), lambda i: (0, 0)),
        ],
        out_specs=[pl.BlockSpec((ts, r), lambda i: (i, 0))] * 2 + [pl.BlockSpec((r, ts), lambda i: (0, i))] * 2,
        out_shape=[jax.ShapeDtypeStruct((s, r), F32)] * 2 + [jax.ShapeDtypeStruct((r, s), F32)] * 2,
        compiler_params=_params("parallel"),
        name="rope_tables",
    )(positions.reshape(s, 1), positions.reshape(1, s), inv_freq2.reshape(1, r), inv_freq2.reshape(r, 1))


def _lnmod_mm_kernel(x_ref, sc_ref, sh_ref, *rest, mode, prenormed):
    if mode in ("glu", "swiglu"):
        wa_ref, wb_ref, o_ref, h_ref = rest
    else:
        wa_ref, o_ref, h_ref = rest
        wb_ref = None

    @pl.when(pl.program_id(1) == 0)
    def _():
        if prenormed:
            h_ref[...] = x_ref[...]
        else:
            h = _layer_norm_rows(x_ref[...]) * (1.0 + sc_ref[...]) + sh_ref[...]
            h_ref[...] = h.astype(BF16)

    h = h_ref[...]
    a = jnp.dot(h, wa_ref[...], preferred_element_type=F32)
    if mode == "plain":
        o = a
    elif mode == "sigmoid":
        o = _sigmoid(a)
    else:
        b = jnp.dot(h, wb_ref[...], preferred_element_type=F32)
        if mode == "glu":
            o = a * _sigmoid(b)
        else:
            o = a * _sigmoid(a) * b
    o_ref[...] = o.astype(o_ref.dtype)


def _lnmod_mm(x, sc, sh, wa, wb=None, *, mode, out_dtype, tm=512, tn=512, prenormed=False, name):
    s, d = x.shape
    n = wa.shape[1]
    dual = wb is not None
    w_spec = pl.BlockSpec((d, tn), lambda i, j: (0, j))
    vec_spec = pl.BlockSpec((1, d), lambda i, j: (0, 0))
    in_specs = [pl.BlockSpec((tm, d), lambda i, j: (i, 0)), vec_spec, vec_spec, w_spec]
    args = [x, sc, sh, wa]
    if dual:
        in_specs.append(w_spec)
        args.append(wb)
    return pl.pallas_call(
        functools.partial(_lnmod_mm_kernel, mode=mode, prenormed=prenormed),
        grid=(s // tm, n // tn),
        in_specs=in_specs,
        out_specs=pl.BlockSpec((tm, tn), lambda i, j: (i, j)),
        out_shape=jax.ShapeDtypeStruct((s, n), out_dtype),
        scratch_shapes=[pltpu.VMEM((tm, d), BF16)],
        compiler_params=_params("parallel", "arbitrary"),
        name=name,
    )(*args)


def _conv_pw_kernel(prev_ref, cur_ref, wdw_ref, bdw_ref, lng_ref, lnb_ref, wpw_ref, o_ref,
                    ext_ref, conv_ref, act_ref, *, tm, cw, rh):
    i = pl.program_id(0)

    @pl.when(pl.program_id(1) == 0)
    def _():
        d = cur_ref.shape[1]
        halo = prev_ref[...].astype(F32)
        ext_ref[0:CONV_HALO, :] = jnp.where(i > 0, halo, 0.0)
        ext_ref[CONV_HALO:CONV_HALO + tm, :] = cur_ref[...].astype(F32)
        first = CONV_HALO - (CONV_WIDTH - 1)

        def chan_chunk(ci, carry):
            cs = pl.ds(pl.multiple_of(ci * cw, cw), cw)
            for r0 in range(0, tm, rh):
                partial = []
                for res in range(SUBLANES):
                    taps = [k for k in range(CONV_WIDTH) if (first + k) % SUBLANES == res]
                    lo = min(first + k for k in taps) - res
                    hi = max(first + k for k in taps) - res
                    shifted = ext_ref[r0 + lo + res:r0 + hi + res + rh, cs]
                    acc = None
                    for k in taps:
                        off = first + k - res - lo
                        prod = wdw_ref[k:k + 1, cs] * shifted[off:off + rh]
                        acc = prod if acc is None else acc + prod
                    partial.append(acc)
                while len(partial) > 1:
                    partial = [a + b for a, b in zip(partial[0::2], partial[1::2])]
                conv_ref[r0:r0 + rh, cs] = partial[0]
            return carry

        lax.fori_loop(0, d // cw, chan_chunk, 0)
        u = conv_ref[...] + bdw_ref[...]
        u = _layer_norm_rows(u) * lng_ref[...] + lnb_ref[...]
        act_ref[...] = (u * _sigmoid(u)).astype(BF16)

    o_ref[...] = jnp.dot(act_ref[...], wpw_ref[...], preferred_element_type=F32).astype(o_ref.dtype)


def _conv_pw(u, w_dw, b_dw, ln_g, ln_b, w_pw, *, tm=256, tn=512, cw=256, rh=64):
    s, d = u.shape
    n = w_pw.shape[1]
    ratio = tm // CONV_HALO
    vec = pl.BlockSpec((1, d), lambda i, j: (0, 0))
    return pl.pallas_call(
        functools.partial(_conv_pw_kernel, tm=tm, cw=cw, rh=rh),
        grid=(s // tm, n // tn),
        in_specs=[
            pl.BlockSpec((CONV_HALO, d), lambda i, j: (jnp.maximum(i * ratio - 1, 0), 0)),
            pl.BlockSpec((tm, d), lambda i, j: (i, 0)),
            pl.BlockSpec((CONV_WIDTH, d), lambda i, j: (0, 0)),
            vec, vec, vec,
            pl.BlockSpec((d, tn), lambda i, j: (0, j)),
        ],
        out_specs=pl.BlockSpec((tm, tn), lambda i, j: (i, j)),
        out_shape=jax.ShapeDtypeStruct((s, n), BF16),
        scratch_shapes=[
            pltpu.VMEM((CONV_HALO + tm, d), F32),
            pltpu.VMEM((tm, d), F32),
            pltpu.VMEM((tm, d), BF16),
        ],
        compiler_params=_params("parallel", "arbitrary"),
        name="conv_pw",
    )(u, u, w_dw, b_dw, ln_g, ln_b, w_pw)


def _rms_rows(x, g):
    return x * lax.rsqrt(jnp.mean(x * x, axis=-1, keepdims=True) + RMS_EPS) * g


_NT = (((1,), (1,)), ((), ()))


def _qproj_kernel(cq_ref, g_ref, wt_ref, wrot_t_ref, cos_t_ref, sin_t_ref, o_ref, *, scale):
    cn = _rms_rows(cq_ref[...], g_ref[...]).astype(BF16)
    main_t = lax.dot_general(wt_ref[...], cn, _NT, preferred_element_type=F32)
    rot_t = lax.dot_general(wrot_t_ref[...], cn, _NT, preferred_element_type=F32)
    cos_t = cos_t_ref[...]
    sin_t = sin_t_ref[...]
    n_nope = N_HEADS * QK_NOPE_DIM
    for h in range(N_HEADS):
        nope = main_t[h * QK_NOPE_DIM:(h + 1) * QK_NOPE_DIM]
        r0 = h * QK_ROPE_DIM
        rope = main_t[n_nope + r0:n_nope + r0 + QK_ROPE_DIM] * cos_t + rot_t[r0:r0 + QK_ROPE_DIM] * sin_t
        o_ref[h, 0:QK_NOPE_DIM, :] = (nope * scale).astype(BF16)
        o_ref[h, QK_NOPE_DIM:QK_HEAD_DIM, :] = (rope * scale).astype(BF16)


def _qproj(lat, g_q, wq_t, wq_rot_t, cos_t, sin_t, *, tm=256):
    s = lat.shape[0]
    scale = QK_HEAD_DIM ** -0.5 * 1.4426950408889634
    return pl.pallas_call(
        functools.partial(_qproj_kernel, scale=scale),
        grid=(s // tm,),
        in_specs=[
            pl.BlockSpec((tm, Q_LORA), lambda i: (i, 0)),
            pl.BlockSpec((1, Q_LORA), lambda i: (0, 0)),
            pl.BlockSpec(wq_t.shape, lambda i: (0, 0)),
            pl.BlockSpec(wq_rot_t.shape, lambda i: (0, 0)),
            pl.BlockSpec((QK_ROPE_DIM, tm), lambda i: (0, i)),
            pl.BlockSpec((QK_ROPE_DIM, tm), lambda i: (0, i)),
        ],
        out_specs=pl.BlockSpec((N_HEADS, QK_HEAD_DIM, tm), lambda i: (0, 0, i)),
        out_shape=jax.ShapeDtypeStruct((N_HEADS, QK_HEAD_DIM, s), BF16),
        compiler_params=_params("parallel"),
        name="q_proj",
    )(lat, g_q, wq_t, wq_rot_t, cos_t, sin_t)


def _kvproj_kernel(ckv_ref, kr_ref, g_ref, wk_ref, wv_t_ref, cos_ref, sin_ref, k_ref, vt_ref):
    cn = _rms_rows(ckv_ref[...], g_ref[...]).astype(BF16)
    kr = kr_ref[:, 0:QK_ROPE_DIM]
    kr_rot = kr_ref[:, QK_ROPE_DIM:2 * QK_ROPE_DIM]
    krope = (kr * cos_ref[...] + kr_rot * sin_ref[...]).astype(BF16)
    k_all = jnp.dot(cn, wk_ref[...], preferred_element_type=F32)
    v_t = lax.dot_general(wv_t_ref[...], cn, _NT, preferred_element_type=F32)
    for h in range(N_HEADS):
        k_ref[h, :, 0:QK_NOPE_DIM] = k_all[:, h * QK_NOPE_DIM:(h + 1) * QK_NOPE_DIM].astype(BF16)
        k_ref[h, :, QK_NOPE_DIM:QK_HEAD_DIM] = krope
        vt_ref[h] = v_t[h * V_HEAD_DIM:(h + 1) * V_HEAD_DIM].astype(BF16)


def _kvproj(lat, g_kv, wk, wv_t, cos, sin, *, tm=256):
    s = lat.shape[0]
    kr_block = (Q_LORA + KV_LORA) // (2 * QK_ROPE_DIM)
    return pl.pallas_call(
        _kvproj_kernel,
        grid=(s // tm,),
        in_specs=[
            pl.BlockSpec((tm, KV_LORA), lambda i: (i, Q_LORA // KV_LORA)),
            pl.BlockSpec((tm, 2 * QK_ROPE_DIM), lambda i: (i, kr_block)),
            pl.BlockSpec((1, KV_LORA), lambda i: (0, 0)),
            pl.BlockSpec(wk.shape, lambda i: (0, 0)),
            pl.BlockSpec(wv_t.shape, lambda i: (0, 0)),
            pl.BlockSpec((tm, QK_ROPE_DIM), lambda i: (i, 0)),
            pl.BlockSpec((tm, QK_ROPE_DIM), lambda i: (i, 0)),
        ],
        out_specs=[
            pl.BlockSpec((N_HEADS, tm, QK_HEAD_DIM), lambda i: (0, i, 0)),
            pl.BlockSpec((N_HEADS, V_HEAD_DIM, tm), lambda i: (0, 0, i)),
        ],
        out_shape=[
            jax.ShapeDtypeStruct((N_HEADS, s, QK_HEAD_DIM), BF16),
            jax.ShapeDtypeStruct((N_HEADS, V_HEAD_DIM, s), BF16),
        ],
        compiler_params=_params("parallel"),
        name="kv_proj",
    )(lat, lat, g_kv, wk, wv_t, cos, sin)


def _flash_kernel(qt_ref, k_ref, vt_ref, o_ref, sa_ref, sb_ref, m_ref, l_ref, acc_ref, *, t):
    qi = pl.program_id(1)
    qt = qt_ref[0]

    def scores(j, s_ref):
        start = pl.multiple_of(j * t, t)
        s_ref[...] = jnp.dot(k_ref[0, pl.ds(start, t), :], qt, preferred_element_type=F32)

    def consume(j, s_ref, mask):
        s = s_ref[...]
        if mask is not None:
            s = jnp.where(mask, s, -jnp.inf)
        m_prev = m_ref[...]
        m_new = jnp.maximum(m_prev, jnp.max(s, axis=0, keepdims=True))
        alpha = jnp.exp2(m_prev - m_new)
        p = jnp.exp2(s - m_new)
        l_ref[...] = alpha * l_ref[...] + jnp.sum(p, axis=0, keepdims=True)
        start = pl.multiple_of(j * t, t)
        pv = jnp.dot(vt_ref[0, :, pl.ds(start, t)], p.astype(BF16), preferred_element_type=F32)
        acc_ref[...] = alpha * acc_ref[...] + pv
        m_ref[...] = m_new

    m_ref[...] = jnp.full(m_ref.shape, -jnp.inf, F32)
    l_ref[...] = jnp.zeros(l_ref.shape, F32)
    acc_ref[...] = jnp.zeros(acc_ref.shape, F32)

    scores(0, sa_ref)

    def pair(tt, carry):
        j = 2 * tt
        scores(j + 1, sb_ref)
        consume(j, sa_ref, None)
        scores(j + 2, sa_ref)
        consume(j + 1, sb_ref, None)
        return carry

    lax.fori_loop(0, qi // 2, pair, 0)

    key_chunk = lax.broadcasted_iota(jnp.int32, (t, t), 0) // CHUNK
    query_chunk = lax.broadcasted_iota(jnp.int32, (t, t), 1) // CHUNK
    diag_mask = key_chunk <= query_chunk

    @pl.when(qi % 2 == 0)
    def _():
        consume(qi, sa_ref, diag_mask)

    @pl.when(qi % 2 == 1)
    def _():
        scores(qi, sb_ref)
        consume(qi - 1, sa_ref, None)
        consume(qi, sb_ref, diag_mask)

    out_t = acc_ref[...] / l_ref[...]
    o_ref[...] = out_t.T.astype(o_ref.dtype)


def _flash(q_t, k, v_t, *, t=512):
    h, _, s = q_t.shape
    return pl.pallas_call(
        functools.partial(_flash_kernel, t=t),
        grid=(h, s // t),
        in_specs=[
            pl.BlockSpec((1, QK_HEAD_DIM, t), lambda hh, i: (hh, 0, i)),
            pl.BlockSpec((1, s, QK_HEAD_DIM), lambda hh, i: (hh, 0, 0)),
            pl.BlockSpec((1, V_HEAD_DIM, s), lambda hh, i: (hh, 0, 0)),
        ],
        out_specs=pl.BlockSpec((t, V_HEAD_DIM), lambda hh, i: (i, hh)),
        out_shape=jax.ShapeDtypeStruct((s, h * V_HEAD_DIM), BF16),
        scratch_shapes=[
            pltpu.VMEM((t, t), F32),
            pltpu.VMEM((t, t), F32),
            pltpu.VMEM((1, t), F32),
            pltpu.VMEM((1, t), F32),
            pltpu.VMEM((V_HEAD_DIM, t), F32),
        ],
        compiler_params=_params("parallel", "arbitrary"),
        name="flash_attn",
    )(q_t, k, v_t)


def _wo_mix_kernel(o_ref, w_ref, gc_ref, gm_ref, yc_ref, out_ref):
    y_mla = jnp.dot(o_ref[...], w_ref[...], preferred_element_type=F32)
    mixed = gc_ref[...].astype(F32) * yc_ref[...].astype(F32) + gm_ref[...].astype(F32) * y_mla
    out_ref[...] = mixed.astype(out_ref.dtype)


def _wo_mix(o, w_o, gates, y_conv, *, tm=512, tn=512):
    s, d = o.shape
    n = w_o.shape[1]
    nb = n // tn
    return pl.pallas_call(
        _wo_mix_kernel,
        grid=(s // tm, nb),
        in_specs=[
            pl.BlockSpec((tm, d), lambda i, j: (i, 0)),
            pl.BlockSpec((d, tn), lambda i, j: (0, j)),
            pl.BlockSpec((tm, tn), lambda i, j: (i, j)),
            pl.BlockSpec((tm, tn), lambda i, j: (i, j + nb)),
            pl.BlockSpec((tm, tn), lambda i, j: (i, j)),
        ],
        out_specs=pl.BlockSpec((tm, tn), lambda i, j: (i, j)),
        out_shape=jax.ShapeDtypeStruct((s, n), BF16),
        compiler_params=_params("parallel", "parallel"),
        name="wo_mix",
    )(o, w_o, gates, gates, y_conv)


def _resid_ln(x, f, gate, lng, lnb):
    y = DEEPNORM_ALPHA * x + (1.0 + gate) * f
    return _layer_norm_rows(y) * lng + lnb


def _mm_resid_ln_kernel(a_ref, w_ref, x_ref, gate_ref, lng_ref, lnb_ref, o_ref, acc_ref):
    kk = pl.program_id(1)

    @pl.when(kk == 0)
    def _():
        acc_ref[...] = jnp.zeros(acc_ref.shape, F32)

    acc_ref[...] += jnp.dot(a_ref[...], w_ref[...], preferred_element_type=F32)

    @pl.when(kk == pl.num_programs(1) - 1)
    def _():
        o_ref[...] = _resid_ln(x_ref[...], acc_ref[...], gate_ref[...], lng_ref[...], lnb_ref[...])


def _mm_resid_ln(a, w, x, gate, lng, lnb, *, tm=512, tk=512, name):
    s, kdim = a.shape
    d = w.shape[1]
    vec = pl.BlockSpec((1, d), lambda i, k: (0, 0))
    return pl.pallas_call(
        _mm_resid_ln_kernel,
        grid=(s // tm, kdim // tk),
        in_specs=[
            pl.BlockSpec((tm, tk), lambda i, k: (i, k)),
            pl.BlockSpec((tk, d), lambda i, k: (k, 0)),
            pl.BlockSpec((tm, d), lambda i, k: (i, 0)),
            vec, vec, vec,
        ],
        out_specs=pl.BlockSpec((tm, d), lambda i, k: (i, 0)),
        out_shape=jax.ShapeDtypeStruct((s, d), F32),
        scratch_shapes=[pltpu.VMEM((tm, d), F32)],
        compiler_params=_params("parallel", "arbitrary"),
        name=name,
    )(a, w, x, gate, lng, lnb)


def _router_kernel(x_ref, sc_ref, sh_ref, wr_ref, h_ref, comb_ref, sel_ref):
    h = _layer_norm_rows(x_ref[...]) * (1.0 + sc_ref[...]) + sh_ref[...]
    h_ref[...] = h
    logits = jnp.dot(h, wr_ref[...], preferred_element_type=F32, precision=lax.Precision.HIGHEST)
    lane = lax.broadcasted_iota(jnp.int32, logits.shape, 1)
    neg = -jnp.inf
    l1 = jnp.where(lane < N_EXPERTS, logits, neg)
    v1 = jnp.max(l1, axis=1, keepdims=True)
    i1 = jnp.min(jnp.where(l1 == v1, lane, LANES), axis=1, keepdims=True)
    l2 = jnp.where(lane == i1, neg, l1)
    v2 = jnp.max(l2, axis=1, keepdims=True)
    i2 = jnp.min(jnp.where(l2 == v2, lane, LANES), axis=1, keepdims=True)
    e2 = jnp.exp(v2 - v1)
    w1 = 1.0 / (1.0 + e2)
    w2 = e2 / (1.0 + e2)
    comb_ref[...] = jnp.where(lane == i1, w1, 0.0) + jnp.where(lane == i2, w2, 0.0)
    sel_ref[...] = jnp.where(lane == i1, 1.0, 0.0) + jnp.where(lane == i2, 2.0, 0.0)


def _router(x, sc, sh, wr_pad, *, tm=256):
    s, d = x.shape
    vec = pl.BlockSpec((1, d), lambda i: (0, 0))
    lane_blk = pl.BlockSpec((tm, LANES), lambda i: (i, 0))
    return pl.pallas_call(
        _router_kernel,
        grid=(s // tm,),
        in_specs=[pl.BlockSpec((tm, d), lambda i: (i, 0)), vec, vec,
                  pl.BlockSpec((d, LANES), lambda i: (0, 0))],
        out_specs=[pl.BlockSpec((tm, d), lambda i: (i, 0)), lane_blk, lane_blk],
        out_shape=[jax.ShapeDtypeStruct((s, d), F32), jax.ShapeDtypeStruct((s, LANES), F32),
                   jax.ShapeDtypeStruct((s, LANES), F32)],
        compiler_params=_params("parallel"),
        name="router",
    )(x, sc, sh, wr_pad)


def _routing_plan(comb, sel, tm, n_tiles):
    sel8 = sel[:, :N_EXPERTS]
    comb8 = comb[:, :N_EXPERTS]
    chosen = (sel8 > 0).astype(jnp.int32)
    rank = jnp.cumsum(chosen, axis=0) - chosen
    counts = jnp.sum(chosen, axis=0)
    tiles_e = (counts + tm - 1) // tm
    tile_end = jnp.cumsum(tiles_e)
    row_start = (tile_end - tiles_e) * tm
    dest = row_start[None, :] + rank
    pos0 = jnp.sum(jnp.where(sel8 == 1.0, dest, 0), axis=1).astype(jnp.int32)
    pos1 = jnp.sum(jnp.where(sel8 == 2.0, dest, 0), axis=1).astype(jnp.int32)
    w0 = jnp.sum(jnp.where(sel8 == 1.0, comb8, 0.0), axis=1, keepdims=True)
    w1 = jnp.sum(jnp.where(sel8 == 2.0, comb8, 0.0), axis=1, keepdims=True)
    n_valid = tile_end[-1]
    tile_id = jnp.minimum(jnp.arange(n_tiles, dtype=jnp.int32), n_valid - 1)
    tile_expert = jnp.sum((tile_id[:, None] >= tile_end[None, :]).astype(jnp.int32), axis=1)
    meta = jnp.concatenate([tile_expert, n_valid[None]]).astype(jnp.int32)
    return pos0, pos1, w0, w1, meta


def _row_copy(src_hbm, dst_hbm, src_row, dst_row, sem):
    return pltpu.make_async_copy(src_hbm.at[pl.ds(src_row, 1)], dst_hbm.at[pl.ds(dst_row, 1)], sem)


def _dispatch_kernel(pos0_ref, pos1_ref, h_hbm, init_hbm, out_hbm, sem, *, tb):
    del init_hbm
    base = pl.program_id(0) * tb

    def issue(r, carry):
        _row_copy(h_hbm, out_hbm, base + r, pos0_ref[0, 0, r], sem).start()
        _row_copy(h_hbm, out_hbm, base + r, pos1_ref[0, 0, r], sem).start()
        return carry

    lax.fori_loop(0, tb, issue, 0)

    def drain(r, carry):
        _row_copy(h_hbm, out_hbm, 0, 0, sem).wait()
        _row_copy(h_hbm, out_hbm, 0, 0, sem).wait()
        return carry

    lax.fori_loop(0, tb, drain, 0)


def _dispatch(h, pos0, pos1, n_rows, *, tb=256):
    s, d = h.shape
    idx_blk = pl.BlockSpec((1, 1, tb), lambda i: (i, 0, 0), memory_space=pltpu.SMEM)
    any_spec = pl.BlockSpec(memory_space=pl.ANY)
    return pl.pallas_call(
        functools.partial(_dispatch_kernel, tb=tb),
        grid=(s // tb,),
        in_specs=[idx_blk, idx_blk, any_spec, any_spec],
        out_specs=any_spec,
        out_shape=jax.ShapeDtypeStruct((n_rows, d), h.dtype),
        scratch_shapes=[pltpu.SemaphoreType.DMA(())],
        input_output_aliases={3: 0},
        compiler_params=_params("arbitrary"),
        name="moe_dispatch",
    )(pos0.reshape(s // tb, 1, tb), pos1.reshape(s // tb, 1, tb), h, jnp.zeros((n_rows, d), h.dtype))


def _expert_swiglu_kernel(meta_ref, x_ref, wg_ref, wu_ref, o_ref, *, n_tiles):
    i = pl.program_id(1)

    @pl.when(i < meta_ref[n_tiles])
    def _():
        x = x_ref[...].astype(BF16)
        a = jnp.dot(x, wg_ref[0], preferred_element_type=F32)
        b = jnp.dot(x, wu_ref[0], preferred_element_type=F32)
        o_ref[...] = (a * _sigmoid(a) * b).astype(o_ref.dtype)

    @pl.when(i >= meta_ref[n_tiles])
    def _():
        o_ref[...] = jnp.zeros(o_ref.shape, o_ref.dtype)


def _expert_swiglu(meta, xs, wg, wu, *, tm, tn=1024):
    r, d = xs.shape
    f = wg.shape[2]
    n_tiles = r // tm

    def row_tile(i, m):
        return jnp.minimum(i, m[n_tiles] - 1)

    return pl.pallas_call(
        functools.partial(_expert_swiglu_kernel, n_tiles=n_tiles),
        grid_spec=pltpu.PrefetchScalarGridSpec(
            num_scalar_prefetch=1,
            grid=(f // tn, n_tiles),
            in_specs=[
                pl.BlockSpec((tm, d), lambda j, i, m: (row_tile(i, m), 0)),
                pl.BlockSpec((1, d, tn), lambda j, i, m: (m[i], 0, j)),
                pl.BlockSpec((1, d, tn), lambda j, i, m: (m[i], 0, j)),
            ],
            out_specs=pl.BlockSpec((tm, tn), lambda j, i, m: (i, j)),
        ),
        out_shape=jax.ShapeDtypeStruct((r, f), BF16),
        compiler_params=_params("arbitrary", "arbitrary"),
        name="expert_swiglu",
    )(meta, xs, wg, wu)


def _expert_down_kernel(meta_ref, a_ref, w_ref, o_ref, acc_ref, *, n_tiles):
    i = pl.program_id(0)
    kk = pl.program_id(1)

    @pl.when(i < meta_ref[n_tiles])
    def _():
        @pl.when(kk == 0)
        def _():
            acc_ref[...] = jnp.zeros(acc_ref.shape, F32)

        acc_ref[...] += jnp.dot(a_ref[...], w_ref[0], preferred_element_type=F32)

        @pl.when(kk == pl.num_programs(1) - 1)
        def _():
            o_ref[...] = acc_ref[...]

    @pl.when(jnp.logical_and(i >= meta_ref[n_tiles], kk == 0))
    def _():
        o_ref[...] = jnp.zeros(o_ref.shape, o_ref.dtype)


def _expert_down(meta, act, wd, *, tm, tk=1024):
    r, f = act.shape
    d = wd.shape[2]
    n_tiles = r // tm
    nk = f // tk

    def used(i, m):
        return i < m[n_tiles]

    def row_tile(i, m):
        return jnp.minimum(i, m[n_tiles] - 1)

    def k_tile(i, k, m):
        return jnp.where(used(i, m), k, nk - 1)

    return pl.pallas_call(
        functools.partial(_expert_down_kernel, n_tiles=n_tiles),
        grid_spec=pltpu.PrefetchScalarGridSpec(
            num_scalar_prefetch=1,
            grid=(n_tiles, nk),
            in_specs=[
                pl.BlockSpec((tm, tk), lambda i, k, m: (row_tile(i, m), k_tile(i, k, m))),
                pl.BlockSpec((1, tk, d), lambda i, k, m: (m[i], k_tile(i, k, m), 0)),
            ],
            out_specs=pl.BlockSpec((tm, d), lambda i, k, m: (i, 0)),
            scratch_shapes=[pltpu.VMEM((tm, d), F32)],
        ),
        out_shape=jax.ShapeDtypeStruct((r, d), F32),
        compiler_params=_params("arbitrary", "arbitrary"),
        name="expert_down",
    )(meta, act, wd)


def _combine_ln_kernel(pos0_ref, pos1_ref, ys_hbm, w0_ref, w1_ref, x_ref, gate_ref, lng_ref, lnb_ref, o_ref,
                       a_buf, b_buf, sem, *, tb):
    def issue(r, carry):
        _row_copy(ys_hbm, a_buf, pos0_ref[0, 0, r], r, sem).start()
        _row_copy(ys_hbm, b_buf, pos1_ref[0, 0, r], r, sem).start()
        return carry

    lax.fori_loop(0, tb, issue, 0)

    def drain(r, carry):
        _row_copy(ys_hbm, a_buf, 0, 0, sem).wait()
        _row_copy(ys_hbm, b_buf, 0, 0, sem).wait()
        return carry

    lax.fori_loop(0, tb, drain, 0)
    f = w0_ref[...] * a_buf[...] + w1_ref[...] * b_buf[...]
    o_ref[...] = _resid_ln(x_ref[...], f, gate_ref[...], lng_ref[...], lnb_ref[...])


def _combine_ln(ys, pos0, pos1, w0, w1, x, gate, lng, lnb, *, tb=256):
    s, d = x.shape
    idx_blk = pl.BlockSpec((1, 1, tb), lambda i: (i, 0, 0), memory_space=pltpu.SMEM)
    vec = pl.BlockSpec((1, d), lambda i: (0, 0))
    col = pl.BlockSpec((tb, 1), lambda i: (i, 0))
    blk = pl.BlockSpec((tb, d), lambda i: (i, 0))
    return pl.pallas_call(
        functools.partial(_combine_ln_kernel, tb=tb),
        grid=(s // tb,),
        in_specs=[idx_blk, idx_blk, pl.BlockSpec(memory_space=pl.ANY), col, col, blk, vec, vec, vec],
        out_specs=blk,
        out_shape=jax.ShapeDtypeStruct((s, d), F32),
        scratch_shapes=[pltpu.VMEM((tb, d), F32), pltpu.VMEM((tb, d), F32), pltpu.SemaphoreType.DMA(())],
        compiler_params=_params("arbitrary"),
        name="moe_combine_ln",
    )(pos0.reshape(s // tb, 1, tb), pos1.reshape(s // tb, 1, tb), ys, w0, w1, x, gate, lng, lnb)


def _rotate_half_cols(w):
    half = w.shape[-1] // 2
    return jnp.concatenate([-w[..., half:], w[..., :half]], axis=-1)


def kernel(x, c, positions, w_ada, b_ada, w_in, g_q, w_uq, g_kv, w_ukv, w_o, w_dw, b_dw, conv_ln_g, conv_ln_b, w_pw, w_out, ln1_g, ln1_b, ln2_g, ln2_b, w_ff_gate, w_ff_up, w_ff_down, w_router, w_e_gate, w_e_up, w_e_down):
    assert x.shape == (1, SEQ, D_MODEL) and c.shape == (1, D_MODEL)
    d = D_MODEL
    xs = x.reshape(SEQ, d)

    mod = _ada(c.reshape(d, 1), w_ada, b_ada).reshape(DEPTH, 6, 1, d)

    inv_freq = ROPE_THETA ** (-jnp.arange(0, QK_ROPE_DIM, 2, dtype=F32) / QK_ROPE_DIM)
    inv_freq2 = jnp.concatenate([inv_freq, inv_freq])
    cos, sin, cos_t, sin_t = _rope_tables(positions.reshape(SEQ), inv_freq2)

    row = lambda v: v.reshape(1, -1)

    for l in range(DEPTH):
        sh1, sc1, g1, sh2, sc2, g2 = (mod[l, t] for t in range(6))
        wl = w_in[l]
        o_kr = Q_LORA + KV_LORA
        o_val = o_kr + QK_ROPE_DIM
        w_kr = wl[:, o_kr:o_val]
        w_lat = jnp.concatenate([wl[:, :o_val], _rotate_half_cols(w_kr)], axis=1).astype(BF16)
        w_val = wl[:, o_val:o_val + d].astype(BF16)
        w_gate = wl[:, o_val + d:o_val + 2 * d].astype(BF16)
        w_brg = wl[:, o_val + 2 * d:].astype(BF16)

        lat = _lnmod_mm(xs, sc1, sh1, w_lat, mode="plain", out_dtype=F32, tn=LATENT_COLS // 3, name="in_latent")
        u = _lnmod_mm(xs, sc1, sh1, w_val, w_gate, mode="glu", out_dtype=BF16, name="in_glu")
        gates = _lnmod_mm(xs, sc1, sh1, w_brg, mode="sigmoid", out_dtype=BF16, name="in_gates")

        y_conv = _conv_pw(u, w_dw[l], row(b_dw[l]), row(conv_ln_g[l]), row(conv_ln_b[l]), w_pw[l].astype(BF16))

        wq = w_uq[l].reshape(Q_LORA, N_HEADS, QK_HEAD_DIM)
        wq_rope = wq[..., QK_NOPE_DIM:]
        wq_t = jnp.concatenate([wq[..., :QK_NOPE_DIM].reshape(Q_LORA, -1), wq_rope.reshape(Q_LORA, -1)], axis=1).T
        wq_rot_t = _rotate_half_cols(wq_rope).reshape(Q_LORA, -1).T
        wkv = w_ukv[l].reshape(KV_LORA, N_HEADS, QK_NOPE_DIM + V_HEAD_DIM)
        wk = wkv[..., :QK_NOPE_DIM].reshape(KV_LORA, -1)
        wv_t = wkv[..., QK_NOPE_DIM:].reshape(KV_LORA, -1).T
        q_t = _qproj(lat, row(g_q[l]), wq_t.astype(BF16), wq_rot_t.astype(BF16), cos_t, sin_t)
        k, v_t = _kvproj(lat, row(g_kv[l]), wk.astype(BF16), wv_t.astype(BF16), cos, sin)
        o = _flash(q_t, k, v_t)
        mixed = _wo_mix(o, w_o[l].astype(BF16), gates, y_conv)
        xs = _mm_resid_ln(mixed, w_out[l].astype(BF16), xs, g1, row(ln1_g[l]), row(ln1_b[l]), name="out_proj_ln")

        i = l // 2
        if l % 2 == 0:
            act = _lnmod_mm(xs, sc2, sh2, w_ff_gate[i].astype(BF16), w_ff_up[i].astype(BF16),
                            mode="swiglu", out_dtype=BF16, name="ffn_swiglu")
            xs = _mm_resid_ln(act, w_ff_down[i].astype(BF16), xs, g2, row(ln2_g[l]), row(ln2_b[l]), name="ffn_down_ln")
        else:
            wr_pad = jnp.zeros((d, LANES), F32).at[:, :N_EXPERTS].set(w_router[i])
            h2, comb, sel = _router(xs, sc2, sh2, wr_pad)
            tm_e = EXPERT_ROW_TILE
            n_tiles = (2 * SEQ) // tm_e + N_EXPERTS
            pos0, pos1, w0, w1, meta = _routing_plan(comb, sel, tm_e, n_tiles)
            x_sorted = _dispatch(h2, pos0, pos1, n_tiles * tm_e)
            act = _expert_swiglu(meta, x_sorted, w_e_gate[i].astype(BF16), w_e_up[i].astype(BF16), tm=tm_e)
            y_sorted = _expert_down(meta, act, w_e_down[i].astype(BF16), tm=tm_e)
            xs = _combine_ln(y_sorted, pos0, pos1, w0, w1, xs, g2, row(ln2_g[l]), row(ln2_b[l]))

    return xs.reshape(1, SEQ, D_MODEL)
```

```python
import functools

import jax
import jax.numpy as jnp
from jax import lax
from jax.experimental import pallas as pl
from jax.experimental.pallas import tpu as pltpu

F32 = jnp.float32
BF16 = jnp.bfloat16

D_MODEL = 2048
SEQ = 8192
DEPTH = 2
CHUNK = 64
N_HEADS = 16
QK_NOPE_DIM = 128
QK_ROPE_DIM = 64
V_HEAD_DIM = 128
QK_HEAD_DIM = QK_NOPE_DIM + QK_ROPE_DIM
Q_LORA = 512
KV_LORA = 512
ROPE_THETA = 10000.0
CONV_WIDTH = 31
D_FF_DENSE = 5632
N_EXPERTS = 8
D_FF_EXPERT = 7168
DEEPNORM_ALPHA = (2.0 * DEPTH) ** 0.25
LN_EPS = 1e-5
RMS_EPS = 1e-6
LATENT_COLS = Q_LORA + KV_LORA + 2 * QK_ROPE_DIM
CONV_HALO = 32
LANES = 128
SUBLANES = 8
EXPERT_ROW_TILE = 512

VMEM_LIMIT = 48 * 1024 * 1024
EXPERT_VMEM_LIMIT = 58 * 1024 * 1024


def _params(*sem):
    return pltpu.CompilerParams(dimension_semantics=sem, vmem_limit_bytes=VMEM_LIMIT)


def _layer_norm_rows(x):
    mu = jnp.mean(x, axis=-1, keepdims=True)
    xc = x - mu
    var = jnp.mean(xc * xc, axis=-1, keepdims=True)
    return xc * lax.rsqrt(var + LN_EPS)


def _sigmoid(x):
    return 1.0 / (1.0 + jnp.exp(-x))


def _ada_kernel(c_ref, w_ref, b_ref, o_ref, *, tk):
    k = pl.program_id(1)

    @pl.when(k == 0)
    def _():
        o_ref[0] = b_ref[0]

    c = c_ref[pl.ds(pl.multiple_of(k * tk, tk), tk), :]
    c_act = c * _sigmoid(c)
    o_ref[0] += jnp.sum(c_act * w_ref[0], axis=0, keepdims=True)


def _ada(c_col, w_ada, b_ada, tk=256):
    depth, d, n = w_ada.shape
    return pl.pallas_call(
        functools.partial(_ada_kernel, tk=tk),
        grid=(depth, d // tk),
        in_specs=[
            pl.BlockSpec((d, 1), lambda l, k: (0, 0)),
            pl.BlockSpec((1, tk, n), lambda l, k: (l, k, 0)),
            pl.BlockSpec((1, 1, n), lambda l, k: (l, 0, 0)),
        ],
        out_specs=pl.BlockSpec((1, 1, n), lambda l, k: (l, 0, 0)),
        out_shape=jax.ShapeDtypeStruct((depth, 1, n), F32),
        compiler_params=_params("parallel", "arbitrary"),
        name="ada_mod",
    )(c_col, w_ada, b_ada.reshape(depth, 1, n))


def _rope_kernel(pos_col_ref, pos_row_ref, inv_row_ref, inv_col_ref, cos_ref, sin_ref, cos_t_ref, sin_t_ref):
    ang = pos_col_ref[...].astype(F32) * inv_row_ref[...]
    cos_ref[...] = jnp.cos(ang)
    sin_ref[...] = jnp.sin(ang)
    ang_t = inv_col_ref[...] * pos_row_ref[...].astype(F32)
    cos_t_ref[...] = jnp.cos(ang_t)
    sin_t_ref[...] = jnp.sin(ang_t)


def _rope_tables(positions, inv_freq2, ts=512):
    s = positions.shape[0]
    r = inv_freq2.shape[0]
    return pl.pallas_call(
        _rope_kernel,
        grid=(s // ts,),
        in_specs=[
            pl.BlockSpec((ts, 1), lambda i: (i, 0)),
            pl.BlockSpec((1, ts), lambda i: (0, i)),
            pl.BlockSpec((1, r), lambda i: (0, 0)),
            pl.BlockSpec((r, 1), lambda i: (0, 0)),
        ],
        out_specs=[pl.BlockSpec((ts, r), lambda i: (i, 0))] * 2 + [pl.BlockSpec((r, ts), lambda i: (0, i))] * 2,
        out_shape=[jax.ShapeDtypeStruct((s, r), F32)] * 2 + [jax.ShapeDtypeStruct((r, s), F32)] * 2,
        compiler_params=_params("parallel"),
        name="rope_tables",
    )(positions.reshape(s, 1), positions.reshape(1, s), inv_freq2.reshape(1, r), inv_freq2.reshape(r, 1))


def _lnmod_mm_kernel(x_ref, sc_ref, sh_ref, *rest, mode):
    if mode in ("glu", "swiglu"):
        wa_ref, wb_ref, o_ref, h_ref = rest
    else:
        wa_ref, o_ref, h_ref = rest
        wb_ref = None

    @pl.when(pl.program_id(1) == 0)
    def _():
        h = _layer_norm_rows(x_ref[...]) * (1.0 + sc_ref[...]) + sh_ref[...]
        h_ref[...] = h.astype(BF16)

    h = h_ref[...]
    a = jnp.dot(h, wa_ref[...], preferred_element_type=F32)
    if mode == "plain":
        o = a
    elif mode == "sigmoid":
        o = _sigmoid(a)
    else:
        b = jnp.dot(h, wb_ref[...], preferred_element_type=F32)
        if mode == "glu":
            o = a * _sigmoid(b)
        else:
            o = a * _sigmoid(a) * b
    o_ref[...] = o.astype(o_ref.dtype)


def _lnmod_mm(x, sc, sh, wa, wb=None, *, mode, out_dtype, tm=1024, tn=512, name):
    s, d = x.shape
    n = wa.shape[1]
    dual = wb is not None
    w_spec = pl.BlockSpec((d, tn), lambda i, j: (0, j))
    vec_spec = pl.BlockSpec((1, d), lambda i, j: (0, 0))
    in_specs = [pl.BlockSpec((tm, d), lambda i, j: (i, 0)), vec_spec, vec_spec, w_spec]
    args = [x, sc, sh, wa]
    if dual:
        in_specs.append(w_spec)
        args.append(wb)
    return pl.pallas_call(
        functools.partial(_lnmod_mm_kernel, mode=mode),
        grid=(s // tm, n // tn),
        in_specs=in_specs,
        out_specs=pl.BlockSpec((tm, tn), lambda i, j: (i, j)),
        out_shape=jax.ShapeDtypeStruct((s, n), out_dtype),
        scratch_shapes=[pltpu.VMEM((tm, d), BF16)],
        compiler_params=_params("parallel", "arbitrary"),
        name=name,
    )(*args)


def _conv_pw_kernel(prev_ref, cur_ref, wdw_ref, bdw_ref, lng_ref, lnb_ref, wpw_ref, o_ref,
                    ext_ref, conv_ref, act_ref, *, tm, cw, rh):
    i = pl.program_id(0)

    @pl.when(pl.program_id(1) == 0)
    def _():
        d = cur_ref.shape[1]
        halo = prev_ref[...].astype(F32)
        ext_ref[0:CONV_HALO, :] = jnp.where(i > 0, halo, 0.0)
        ext_ref[CONV_HALO:CONV_HALO + tm, :] = cur_ref[...].astype(F32)
        first = CONV_HALO - (CONV_WIDTH - 1)

        def chan_chunk(ci, carry):
            cs = pl.ds(pl.multiple_of(ci * cw, cw), cw)
            for r0 in range(0, tm, rh):
                partial = []
                for res in range(SUBLANES):
                    taps = [k for k in range(CONV_WIDTH) if (first + k) % SUBLANES == res]
                    lo = min(first + k for k in taps) - res
                    hi = max(first + k for k in taps) - res
                    shifted = ext_ref[r0 + lo + res:r0 + hi + res + rh, cs]
                    acc = None
                    for k in taps:
                        off = first + k - res - lo
                        prod = wdw_ref[k:k + 1, cs] * shifted[off:off + rh]
                        acc = prod if acc is None else acc + prod
                    partial.append(acc)
                while len(partial) > 1:
                    partial = [a + b for a, b in zip(partial[0::2], partial[1::2])]
                conv_ref[r0:r0 + rh, cs] = partial[0]
            return carry

        lax.fori_loop(0, d // cw, chan_chunk, 0)
        u = conv_ref[...] + bdw_ref[...]
        u = _layer_norm_rows(u) * lng_ref[...] + lnb_ref[...]
        act_ref[...] = (u * _sigmoid(u)).astype(BF16)

    o_ref[...] = jnp.dot(act_ref[...], wpw_ref[...], preferred_element_type=F32).astype(o_ref.dtype)


def _conv_pw(u, w_dw, b_dw, ln_g, ln_b, w_pw, *, tm=256, tn=2048, cw=256, rh=64):
    s, d = u.shape
    n = w_pw.shape[1]
    ratio = tm // CONV_HALO
    vec = pl.BlockSpec((1, d), lambda i, j: (0, 0))
    return pl.pallas_call(
        functools.partial(_conv_pw_kernel, tm=tm, cw=cw, rh=rh),
        grid=(s // tm, n // tn),
        in_specs=[
            pl.BlockSpec((CONV_HALO, d), lambda i, j: (jnp.maximum(i * ratio - 1, 0), 0)),
            pl.BlockSpec((tm, d), lambda i, j: (i, 0)),
            pl.BlockSpec((CONV_WIDTH, d), lambda i, j: (0, 0)),
            vec, vec, vec,
            pl.BlockSpec((d, tn), lambda i, j: (0, j)),
        ],
        out_specs=pl.BlockSpec((tm, tn), lambda i, j: (i, j)),
        out_shape=jax.ShapeDtypeStruct((s, n), BF16),
        scratch_shapes=[
            pltpu.VMEM((CONV_HALO + tm, d), F32),
            pltpu.VMEM((tm, d), F32),
            pltpu.VMEM((tm, d), BF16),
        ],
        compiler_params=_params("parallel", "arbitrary"),
        name="conv_pw",
    )(u, u, w_dw, b_dw, ln_g, ln_b, w_pw)


def _rms_rows(x, g):
    return x * lax.rsqrt(jnp.mean(x * x, axis=-1, keepdims=True) + RMS_EPS) * g


_NT = (((1,), (1,)), ((), ()))


def _qproj_kernel(cq_ref, g_ref, wt_ref, wrot_t_ref, cos_t_ref, sin_t_ref, o_ref, *, scale):
    cn = _rms_rows(cq_ref[...], g_ref[...]).astype(BF16)
    main_t = lax.dot_general(wt_ref[...], cn, _NT, preferred_element_type=F32)
    rot_t = lax.dot_general(wrot_t_ref[...], cn, _NT, preferred_element_type=F32)
    cos_t = cos_t_ref[...]
    sin_t = sin_t_ref[...]
    n_nope = N_HEADS * QK_NOPE_DIM
    for h in range(N_HEADS):
        nope = main_t[h * QK_NOPE_DIM:(h + 1) * QK_NOPE_DIM]
        r0 = h * QK_ROPE_DIM
        rope = main_t[n_nope + r0:n_nope + r0 + QK_ROPE_DIM] * cos_t + rot_t[r0:r0 + QK_ROPE_DIM] * sin_t
        o_ref[h, 0:QK_NOPE_DIM, :] = (nope * scale).astype(BF16)
        o_ref[h, QK_NOPE_DIM:QK_HEAD_DIM, :] = (rope * scale).astype(BF16)


def _qproj(lat, g_q, wq_t, wq_rot_t, cos_t, sin_t, *, tm=256):
    s = lat.shape[0]
    scale = QK_HEAD_DIM ** -0.5 * 1.4426950408889634
    return pl.pallas_call(
        functools.partial(_qproj_kernel, scale=scale),
        grid=(s // tm,),
        in_specs=[
            pl.BlockSpec((tm, Q_LORA), lambda i: (i, 0)),
            pl.BlockSpec((1, Q_LORA), lambda i: (0, 0)),
            pl.BlockSpec(wq_t.shape, lambda i: (0, 0)),
            pl.BlockSpec(wq_rot_t.shape, lambda i: (0, 0)),
            pl.BlockSpec((QK_ROPE_DIM, tm), lambda i: (0, i)),
            pl.BlockSpec((QK_ROPE_DIM, tm), lambda i: (0, i)),
        ],
        out_specs=pl.BlockSpec((N_HEADS, QK_HEAD_DIM, tm), lambda i: (0, 0, i)),
        out_shape=jax.ShapeDtypeStruct((N_HEADS, QK_HEAD_DIM, s), BF16),
        compiler_params=_params("parallel"),
        name="q_proj",
    )(lat, g_q, wq_t, wq_rot_t, cos_t, sin_t)


def _kvproj_kernel(ckv_ref, kr_ref, g_ref, wk_ref, wv_t_ref, cos_ref, sin_ref, k_ref, vt_ref):
    cn = _rms_rows(ckv_ref[...], g_ref[...]).astype(BF16)
    kr = kr_ref[:, 0:QK_ROPE_DIM]
    kr_rot = kr_ref[:, QK_ROPE_DIM:2 * QK_ROPE_DIM]
    krope = (kr * cos_ref[...] + kr_rot * sin_ref[...]).astype(BF16)
    k_all = jnp.dot(cn, wk_ref[...], preferred_element_type=F32)
    v_t = lax.dot_general(wv_t_ref[...], cn, _NT, preferred_element_type=F32)
    for h in range(N_HEADS):
        k_ref[h, :, 0:QK_NOPE_DIM] = k_all[:, h * QK_NOPE_DIM:(h + 1) * QK_NOPE_DIM].astype(BF16)
        k_ref[h, :, QK_NOPE_DIM:QK_HEAD_DIM] = krope
        vt_ref[h] = v_t[h * V_HEAD_DIM:(h + 1) * V_HEAD_DIM].astype(BF16)


def _kvproj(lat, g_kv, wk, wv_t, cos, sin, *, tm=256):
    s = lat.shape[0]
    kr_block = (Q_LORA + KV_LORA) // (2 * QK_ROPE_DIM)
    return pl.pallas_call(
        _kvproj_kernel,
        grid=(s // tm,),
        in_specs=[
            pl.BlockSpec((tm, KV_LORA), lambda i: (i, Q_LORA // KV_LORA)),
            pl.BlockSpec((tm, 2 * QK_ROPE_DIM), lambda i: (i, kr_block)),
            pl.BlockSpec((1, KV_LORA), lambda i: (0, 0)),
            pl.BlockSpec(wk.shape, lambda i: (0, 0)),
            pl.BlockSpec(wv_t.shape, lambda i: (0, 0)),
            pl.BlockSpec((tm, QK_ROPE_DIM), lambda i: (i, 0)),
            pl.BlockSpec((tm, QK_ROPE_DIM), lambda i: (i, 0)),
        ],
        out_specs=[
            pl.BlockSpec((N_HEADS, tm, QK_HEAD_DIM), lambda i: (0, i, 0)),
            pl.BlockSpec((N_HEADS, V_HEAD_DIM, tm), lambda i: (0, 0, i)),
        ],
        out_shape=[
            jax.ShapeDtypeStruct((N_HEADS, s, QK_HEAD_DIM), BF16),
            jax.ShapeDtypeStruct((N_HEADS, V_HEAD_DIM, s), BF16),
        ],
        compiler_params=_params("parallel"),
        name="kv_proj",
    )(lat, lat, g_kv, wk, wv_t, cos, sin)


def _flash_kernel(qt_ref, k_ref, vt_ref, o_ref, sa_ref, sb_ref, m_ref, l_ref, acc_ref, *, t):
    qi = pl.program_id(1)
    qt = qt_ref[0]

    def scores(j, s_ref):
        start = pl.multiple_of(j * t, t)
        s_ref[...] = jnp.dot(k_ref[0, pl.ds(start, t), :], qt, preferred_element_type=F32)

    def consume(j, s_ref, mask):
        s = s_ref[...]
        if mask is not None:
            s = jnp.where(mask, s, -jnp.inf)
        m_prev = m_ref[...]
        m_new = jnp.maximum(m_prev, jnp.max(s, axis=0, keepdims=True))
        alpha = jnp.exp2(m_prev - m_new)
        p = jnp.exp2(s - m_new)
        l_ref[...] = alpha * l_ref[...] + jnp.sum(p, axis=0, keepdims=True)
        start = pl.multiple_of(j * t, t)
        pv = jnp.dot(vt_ref[0, :, pl.ds(start, t)], p.astype(BF16), preferred_element_type=F32)
        acc_ref[...] = alpha * acc_ref[...] + pv
        m_ref[...] = m_new

    m_ref[...] = jnp.full(m_ref.shape, -jnp.inf, F32)
    l_ref[...] = jnp.zeros(l_ref.shape, F32)
    acc_ref[...] = jnp.zeros(acc_ref.shape, F32)

    scores(0, sa_ref)

    def pair(tt, carry):
        j = 2 * tt
        scores(j + 1, sb_ref)
        consume(j, sa_ref, None)
        scores(j + 2, sa_ref)
        consume(j + 1, sb_ref, None)
        return carry

    lax.fori_loop(0, qi // 2, pair, 0)

    key_chunk = lax.broadcasted_iota(jnp.int32, (t, t), 0) // CHUNK
    query_chunk = lax.broadcasted_iota(jnp.int32, (t, t), 1) // CHUNK
    diag_mask = key_chunk <= query_chunk

    @pl.when(qi % 2 == 0)
    def _():
        consume(qi, sa_ref, diag_mask)

    @pl.when(qi % 2 == 1)
    def _():
        scores(qi, sb_ref)
        consume(qi - 1, sa_ref, None)
        consume(qi, sb_ref, diag_mask)

    out_t = acc_ref[...] / l_ref[...]
    o_ref[...] = out_t.T.astype(o_ref.dtype)


def _flash(q_t, k, v_t, *, t=512):
    h, _, s = q_t.shape
    return pl.pallas_call(
        functools.partial(_flash_kernel, t=t),
        grid=(h, s // t),
        in_specs=[
            pl.BlockSpec((1, QK_HEAD_DIM, t), lambda hh, i: (hh, 0, i)),
            pl.BlockSpec((1, s, QK_HEAD_DIM), lambda hh, i: (hh, 0, 0)),
            pl.BlockSpec((1, V_HEAD_DIM, s), lambda hh, i: (hh, 0, 0)),
        ],
        out_specs=pl.BlockSpec((t, V_HEAD_DIM), lambda hh, i: (i, hh)),
        out_shape=jax.ShapeDtypeStruct((s, h * V_HEAD_DIM), BF16),
        scratch_shapes=[
            pltpu.VMEM((t, t), F32),
            pltpu.VMEM((t, t), F32),
            pltpu.VMEM((1, t), F32),
            pltpu.VMEM((1, t), F32),
            pltpu.VMEM((V_HEAD_DIM, t), F32),
        ],
        compiler_params=_params("parallel", "arbitrary"),
        name="flash_attn",
    )(q_t, k, v_t)


def _wo_mix_kernel(o_ref, w_ref, gc_ref, gm_ref, yc_ref, out_ref):
    y_mla = jnp.dot(o_ref[...], w_ref[...], preferred_element_type=F32)
    mixed = gc_ref[...].astype(F32) * yc_ref[...].astype(F32) + gm_ref[...].astype(F32) * y_mla
    out_ref[...] = mixed.astype(out_ref.dtype)


def _wo_mix(o, w_o, gates, y_conv, *, tm=1024, tn=512):
    s, d = o.shape
    n = w_o.shape[1]
    nb = n // tn
    return pl.pallas_call(
        _wo_mix_kernel,
        grid=(s // tm, nb),
        in_specs=[
            pl.BlockSpec((tm, d), lambda i, j: (i, 0)),
            pl.BlockSpec((d, tn), lambda i, j: (0, j)),
            pl.BlockSpec((tm, tn), lambda i, j: (i, j)),
            pl.BlockSpec((tm, tn), lambda i, j: (i, j + nb)),
            pl.BlockSpec((tm, tn), lambda i, j: (i, j)),
        ],
        out_specs=pl.BlockSpec((tm, tn), lambda i, j: (i, j)),
        out_shape=jax.ShapeDtypeStruct((s, n), BF16),
        compiler_params=_params("parallel", "parallel"),
        name="wo_mix",
    )(o, w_o, gates, gates, y_conv)


def _resid_ln(x, f, gate, lng, lnb):
    y = DEEPNORM_ALPHA * x + (1.0 + gate) * f
    return _layer_norm_rows(y) * lng + lnb


def _mm_resid_ln_kernel(a_ref, w_ref, x_ref, gate_ref, lng_ref, lnb_ref, o_ref, acc_ref):
    kk = pl.program_id(1)

    @pl.when(kk == 0)
    def _():
        acc_ref[...] = jnp.zeros(acc_ref.shape, F32)

    acc_ref[...] += jnp.dot(a_ref[...], w_ref[...], preferred_element_type=F32)

    @pl.when(kk == pl.num_programs(1) - 1)
    def _():
        o_ref[...] = _resid_ln(x_ref[...], acc_ref[...], gate_ref[...], lng_ref[...], lnb_ref[...])


def _mm_resid_ln(a, w, x, gate, lng, lnb, *, tm=512, tk, name):
    s, kdim = a.shape
    d = w.shape[1]
    vec = pl.BlockSpec((1, d), lambda i, k: (0, 0))
    return pl.pallas_call(
        _mm_resid_ln_kernel,
        grid=(s // tm, kdim // tk),
        in_specs=[
            pl.BlockSpec((tm, tk), lambda i, k: (i, k)),
            pl.BlockSpec((tk, d), lambda i, k: (k, 0)),
            pl.BlockSpec((tm, d), lambda i, k: (i, 0)),
            vec, vec, vec,
        ],
        out_specs=pl.BlockSpec((tm, d), lambda i, k: (i, 0)),
        out_shape=jax.ShapeDtypeStruct((s, d), F32),
        scratch_shapes=[pltpu.VMEM((tm, d), F32)],
        compiler_params=_params("parallel", "arbitrary"),
        name=name,
    )(a, w, x, gate, lng, lnb)


def _router_kernel(x_ref, sc_ref, sh_ref, wr_ref, h_ref, comb_ref, sel_ref):
    h = _layer_norm_rows(x_ref[...]) * (1.0 + sc_ref[...]) + sh_ref[...]
    h_ref[...] = h
    logits = jnp.dot(h, wr_ref[...], preferred_element_type=F32, precision=lax.Precision.HIGHEST)
    lane = lax.broadcasted_iota(jnp.int32, logits.shape, 1)
    neg = -jnp.inf
    l1 = jnp.where(lane < N_EXPERTS, logits, neg)
    v1 = jnp.max(l1, axis=1, keepdims=True)
    i1 = jnp.min(jnp.where(l1 == v1, lane, LANES), axis=1, keepdims=True)
    l2 = jnp.where(lane == i1, neg, l1)
    v2 = jnp.max(l2, axis=1, keepdims=True)
    i2 = jnp.min(jnp.where(l2 == v2, lane, LANES), axis=1, keepdims=True)
    e2 = jnp.exp(v2 - v1)
    w1 = 1.0 / (1.0 + e2)
    w2 = e2 / (1.0 + e2)
    comb_ref[...] = jnp.where(lane == i1, w1, 0.0) + jnp.where(lane == i2, w2, 0.0)
    sel_ref[...] = jnp.where(lane == i1, 1.0, 0.0) + jnp.where(lane == i2, 2.0, 0.0)


def _router(x, sc, sh, wr_pad, *, tm=256):
    s, d = x.shape
    vec = pl.BlockSpec((1, d), lambda i: (0, 0))
    lane_blk = pl.BlockSpec((tm, LANES), lambda i: (i, 0))
    return pl.pallas_call(
        _router_kernel,
        grid=(s // tm,),
        in_specs=[pl.BlockSpec((tm, d), lambda i: (i, 0)), vec, vec,
                  pl.BlockSpec((d, LANES), lambda i: (0, 0))],
        out_specs=[pl.BlockSpec((tm, d), lambda i: (i, 0)), lane_blk, lane_blk],
        out_shape=[jax.ShapeDtypeStruct((s, d), F32), jax.ShapeDtypeStruct((s, LANES), F32),
                   jax.ShapeDtypeStruct((s, LANES), F32)],
        compiler_params=_params("parallel"),
        name="router",
    )(x, sc, sh, wr_pad)


def _routing_plan(comb, sel, tm, n_tiles):
    sel8 = sel[:, :N_EXPERTS]
    comb8 = comb[:, :N_EXPERTS]
    chosen = (sel8 > 0).astype(jnp.int32)
    rank = jnp.cumsum(chosen, axis=0) - chosen
    counts = jnp.sum(chosen, axis=0)
    tiles_e = (counts + tm - 1) // tm
    tile_end = jnp.cumsum(tiles_e)
    row_start = (tile_end - tiles_e) * tm
    dest = row_start[None, :] + rank
    pos0 = jnp.sum(jnp.where(sel8 == 1.0, dest, 0), axis=1).astype(jnp.int32)
    pos1 = jnp.sum(jnp.where(sel8 == 2.0, dest, 0), axis=1).astype(jnp.int32)
    w0 = jnp.sum(jnp.where(sel8 == 1.0, comb8, 0.0), axis=1, keepdims=True)
    w1 = jnp.sum(jnp.where(sel8 == 2.0, comb8, 0.0), axis=1, keepdims=True)
    n_valid = tile_end[-1]
    tile_id = jnp.minimum(jnp.arange(n_tiles, dtype=jnp.int32), n_valid - 1)
    tile_expert = jnp.sum((tile_id[:, None] >= tile_end[None, :]).astype(jnp.int32), axis=1)
    meta = jnp.concatenate([tile_expert, n_valid[None]]).astype(jnp.int32)
    tok = jnp.arange(sel.shape[0], dtype=jnp.int32)
    src_token = jnp.zeros((n_tiles * tm,), jnp.int32).at[jnp.concatenate([pos0, pos1])].set(
        jnp.concatenate([tok, tok]), unique_indices=True)
    return pos0, pos1, w0, w1, meta, src_token


def _row_copy(src_hbm, dst_hbm, src_row, dst_row, sem):
    return pltpu.make_async_copy(src_hbm.at[pl.ds(src_row, 1)], dst_hbm.at[pl.ds(dst_row, 1)], sem)


def _dispatch_kernel(meta_ref, src_ref, h_hbm, o_ref, buf, sem, *, tb, n_tiles):
    i = pl.program_id(0)

    @pl.when(i < meta_ref[n_tiles])
    def _():
        def issue(r, carry):
            _row_copy(h_hbm, buf, src_ref[0, 0, r], r, sem).start()
            return carry

        lax.fori_loop(0, tb, issue, 0)

        def drain(r, carry):
            _row_copy(h_hbm, buf, 0, 0, sem).wait()
            return carry

        lax.fori_loop(0, tb, drain, 0)
        o_ref[...] = buf[...].astype(o_ref.dtype)

    @pl.when(i >= meta_ref[n_tiles])
    def _():
        o_ref[...] = jnp.zeros(o_ref.shape, o_ref.dtype)


def _dispatch(meta, src_token, h, *, tb):
    n_rows = src_token.shape[0]
    d = h.shape[1]
    n_tiles = n_rows // tb
    return pl.pallas_call(
        functools.partial(_dispatch_kernel, tb=tb, n_tiles=n_tiles),
        grid_spec=pltpu.PrefetchScalarGridSpec(
            num_scalar_prefetch=1,
            grid=(n_tiles,),
            in_specs=[
                pl.BlockSpec((1, 1, tb), lambda i, m: (i, 0, 0), memory_space=pltpu.SMEM),
                pl.BlockSpec(memory_space=pl.ANY),
            ],
            out_specs=pl.BlockSpec((tb, d), lambda i, m: (i, 0)),
            scratch_shapes=[pltpu.VMEM((tb, d), F32), pltpu.SemaphoreType.DMA(())],
        ),
        out_shape=jax.ShapeDtypeStruct((n_rows, d), BF16),
        compiler_params=_params("arbitrary"),
        name="moe_dispatch",
    )(meta, src_token.reshape(n_tiles, 1, tb), h)


def _expert_swiglu_kernel(meta_ref, x_ref, wg_ref, wu_ref, o_ref, wg_bf, wu_bf, *, n_tiles):
    i = pl.program_id(1)

    @pl.when(jnp.logical_or(i == 0, meta_ref[i] != meta_ref[jnp.maximum(i - 1, 0)]))
    def _():
        wg_bf[...] = wg_ref[0, 0].astype(BF16)
        wu_bf[...] = wu_ref[0, 0].astype(BF16)

    @pl.when(i < meta_ref[n_tiles])
    def _():
        x = x_ref[...]
        a = jnp.dot(x, wg_bf[...], preferred_element_type=F32)
        b = jnp.dot(x, wu_bf[...], preferred_element_type=F32)
        o_ref[...] = (a * _sigmoid(a) * b).astype(o_ref.dtype)

    @pl.when(i >= meta_ref[n_tiles])
    def _():
        o_ref[...] = jnp.zeros(o_ref.shape, o_ref.dtype)


def _expert_swiglu(meta, xs, wg, wu, *, layer, tm, tn=1024):
    r, d = xs.shape
    f = wg.shape[3]
    n_tiles = r // tm
    w_spec = pl.BlockSpec((1, 1, d, tn), lambda j, i, m: (layer, m[i], 0, j))

    def row_tile(i, m):
        return jnp.minimum(i, m[n_tiles] - 1)

    return pl.pallas_call(
        functools.partial(_expert_swiglu_kernel, n_tiles=n_tiles),
        grid_spec=pltpu.PrefetchScalarGridSpec(
            num_scalar_prefetch=1,
            grid=(f // tn, n_tiles),
            in_specs=[
                pl.BlockSpec((tm, d), lambda j, i, m: (row_tile(i, m), 0)),
                w_spec,
                w_spec,
            ],
            out_specs=pl.BlockSpec((tm, tn), lambda j, i, m: (i, j)),
            scratch_shapes=[pltpu.VMEM((d, tn), BF16), pltpu.VMEM((d, tn), BF16)],
        ),
        out_shape=jax.ShapeDtypeStruct((r, f), BF16),
        compiler_params=pltpu.CompilerParams(dimension_semantics=("arbitrary", "arbitrary"),
                                             vmem_limit_bytes=EXPERT_VMEM_LIMIT),
        name="expert_swiglu",
    )(meta, xs, wg, wu)


def _expert_down_kernel(meta_ref, a_ref, w_ref, o_ref, acc_ref, *, n_tiles):
    i = pl.program_id(0)
    kk = pl.program_id(1)

    @pl.when(i < meta_ref[n_tiles])
    def _():
        @pl.when(kk == 0)
        def _():
            acc_ref[...] = jnp.zeros(acc_ref.shape, F32)

        acc_ref[...] += jnp.dot(a_ref[...], w_ref[0], preferred_element_type=F32)

        @pl.when(kk == pl.num_programs(1) - 1)
        def _():
            o_ref[...] = acc_ref[...]

    @pl.when(jnp.logical_and(i >= meta_ref[n_tiles], kk == 0))
    def _():
        o_ref[...] = jnp.zeros(o_ref.shape, o_ref.dtype)


def _expert_down(meta, act, wd, *, tm, tk=1024):
    r, f = act.shape
    d = wd.shape[2]
    n_tiles = r // tm
    nk = f // tk

    def used(i, m):
        return i < m[n_tiles]

    def row_tile(i, m):
        return jnp.minimum(i, m[n_tiles] - 1)

    def k_tile(i, k, m):
        return jnp.where(used(i, m), k, nk - 1)

    return pl.pallas_call(
        functools.partial(_expert_down_kernel, n_tiles=n_tiles),
        grid_spec=pltpu.PrefetchScalarGridSpec(
            num_scalar_prefetch=1,
            grid=(n_tiles, nk),
            in_specs=[
                pl.BlockSpec((tm, tk), lambda i, k, m: (row_tile(i, m), k_tile(i, k, m))),
                pl.BlockSpec((1, tk, d), lambda i, k, m: (m[i], k_tile(i, k, m), 0)),
            ],
            out_specs=pl.BlockSpec((tm, d), lambda i, k, m: (i, 0)),
            scratch_shapes=[pltpu.VMEM((tm, d), F32)],
        ),
        out_shape=jax.ShapeDtypeStruct((r, d), F32),
        compiler_params=_params("arbitrary", "arbitrary"),
        name="expert_down",
    )(meta, act, wd)


def _combine_ln_kernel(pos0_ref, pos1_ref, ys_hbm, w0_ref, w1_ref, x_ref, gate_ref, lng_ref, lnb_ref, o_ref,
                       a_buf, b_buf, sem, *, tb):
    def issue(r, carry):
        _row_copy(ys_hbm, a_buf, pos0_ref[0, 0, r], r, sem).start()
        _row_copy(ys_hbm, b_buf, pos1_ref[0, 0, r], r, sem).start()
        return carry

    lax.fori_loop(0, tb, issue, 0)

    def drain(r, carry):
        _row_copy(ys_hbm, a_buf, 0, 0, sem).wait()
        _row_copy(ys_hbm, b_buf, 0, 0, sem).wait()
        return carry

    lax.fori_loop(0, tb, drain, 0)
    f = w0_ref[...] * a_buf[...] + w1_ref[...] * b_buf[...]
    o_ref[...] = _resid_ln(x_ref[...], f, gate_ref[...], lng_ref[...], lnb_ref[...])


def _combine_ln(ys, pos0, pos1, w0, w1, x, gate, lng, lnb, *, tb=256):
    s, d = x.shape
    idx_blk = pl.BlockSpec((1, 1, tb), lambda i: (i, 0, 0), memory_space=pltpu.SMEM)
    vec = pl.BlockSpec((1, d), lambda i: (0, 0))
    col = pl.BlockSpec((tb, 1), lambda i: (i, 0))
    blk = pl.BlockSpec((tb, d), lambda i: (i, 0))
    return pl.pallas_call(
        functools.partial(_combine_ln_kernel, tb=tb),
        grid=(s // tb,),
        in_specs=[idx_blk, idx_blk, pl.BlockSpec(memory_space=pl.ANY), col, col, blk, vec, vec, vec],
        out_specs=blk,
        out_shape=jax.ShapeDtypeStruct((s, d), F32),
        scratch_shapes=[pltpu.VMEM((tb, d), F32), pltpu.VMEM((tb, d), F32), pltpu.SemaphoreType.DMA(())],
        compiler_params=_params("arbitrary"),
        name="moe_combine_ln",
    )(pos0.reshape(s // tb, 1, tb), pos1.reshape(s // tb, 1, tb), ys, w0, w1, x, gate, lng, lnb)


def _rotate_half_cols(w):
    half = w.shape[-1] // 2
    return jnp.concatenate([-w[..., half:], w[..., :half]], axis=-1)


def kernel(x, c, positions, w_ada, b_ada, w_in, g_q, w_uq, g_kv, w_ukv, w_o, w_dw, b_dw, conv_ln_g, conv_ln_b, w_pw, w_out, ln1_g, ln1_b, ln2_g, ln2_b, w_ff_gate, w_ff_up, w_ff_down, w_router, w_e_gate, w_e_up, w_e_down):
    assert x.shape == (1, SEQ, D_MODEL) and c.shape == (1, D_MODEL)
    d = D_MODEL
    xs = x.reshape(SEQ, d)

    mod = _ada(c.reshape(d, 1), w_ada, b_ada).reshape(DEPTH, 6, 1, d)

    inv_freq = ROPE_THETA ** (-jnp.arange(0, QK_ROPE_DIM, 2, dtype=F32) / QK_ROPE_DIM)
    inv_freq2 = jnp.concatenate([inv_freq, inv_freq])
    cos, sin, cos_t, sin_t = _rope_tables(positions.reshape(SEQ), inv_freq2)

    row = lambda v: v.reshape(1, -1)

    for l in range(DEPTH):
        sh1, sc1, g1, sh2, sc2, g2 = (mod[l, t] for t in range(6))
        wl = w_in[l]
        o_kr = Q_LORA + KV_LORA
        o_val = o_kr + QK_ROPE_DIM
        w_kr = wl[:, o_kr:o_val]
        w_lat = jnp.concatenate([wl[:, :o_val], _rotate_half_cols(w_kr)], axis=1).astype(BF16)
        w_val = wl[:, o_val:o_val + d].astype(BF16)
        w_gate = wl[:, o_val + d:o_val + 2 * d].astype(BF16)
        w_brg = wl[:, o_val + 2 * d:].astype(BF16)

        lat = _lnmod_mm(xs, sc1, sh1, w_lat, mode="plain", out_dtype=F32, tn=LATENT_COLS // 3, name="in_latent")
        u = _lnmod_mm(xs, sc1, sh1, w_val, w_gate, mode="glu", out_dtype=BF16, name="in_glu")
        gates = _lnmod_mm(xs, sc1, sh1, w_brg, mode="sigmoid", out_dtype=BF16, name="in_gates")

        y_conv = _conv_pw(u, w_dw[l], row(b_dw[l]), row(conv_ln_g[l]), row(conv_ln_b[l]), w_pw[l].astype(BF16))

        wq = w_uq[l].reshape(Q_LORA, N_HEADS, QK_HEAD_DIM)
        wq_rope = wq[..., QK_NOPE_DIM:]
        wq_t = jnp.concatenate([wq[..., :QK_NOPE_DIM].reshape(Q_LORA, -1), wq_rope.reshape(Q_LORA, -1)], axis=1).T
        wq_rot_t = _rotate_half_cols(wq_rope).reshape(Q_LORA, -1).T
        wkv = w_ukv[l].reshape(KV_LORA, N_HEADS, QK_NOPE_DIM + V_HEAD_DIM)
        wk = wkv[..., :QK_NOPE_DIM].reshape(KV_LORA, -1)
        wv_t = wkv[..., QK_NOPE_DIM:].reshape(KV_LORA, -1).T
        q_t = _qproj(lat, row(g_q[l]), wq_t.astype(BF16), wq_rot_t.astype(BF16), cos_t, sin_t)
        k, v_t = _kvproj(lat, row(g_kv[l]), wk.astype(BF16), wv_t.astype(BF16), cos, sin)
        o = _flash(q_t, k, v_t)
        mixed = _wo_mix(o, w_o[l].astype(BF16), gates, y_conv)
        xs = _mm_resid_ln(mixed, w_out[l].astype(BF16), xs, g1, row(ln1_g[l]), row(ln1_b[l]), tk=d, name="out_proj_ln")

        i = l // 2
        if l % 2 == 0:
            act = _lnmod_mm(xs, sc2, sh2, w_ff_gate[i].astype(BF16), w_ff_up[i].astype(BF16),
                            mode="swiglu", out_dtype=BF16, name="ffn_swiglu")
            xs = _mm_resid_ln(act, w_ff_down[i].astype(BF16), xs, g2, row(ln2_g[l]), row(ln2_b[l]), tk=D_FF_DENSE // 4, name="ffn_down_ln")
        else:
            wr_pad = jnp.zeros((d, LANES), F32).at[:, :N_EXPERTS].set(w_router[i])
            h2, comb, sel = _router(xs, sc2, sh2, wr_pad)
            tm_e = EXPERT_ROW_TILE
            n_tiles = (2 * SEQ) // tm_e + N_EXPERTS
            pos0, pos1, w0, w1, meta, src_token = _routing_plan(comb, sel, tm_e, n_tiles)
            x_sorted = _dispatch(meta, src_token, h2, tb=tm_e)
            act = _expert_swiglu(meta, x_sorted, w_e_gate, w_e_up, layer=i, tm=tm_e)
            y_sorted = _expert_down(meta, act, w_e_down[i].astype(BF16), tm=tm_e)
            xs = _combine_ln(y_sorted, pos0, pos1, w0, w1, xs, g2, row(ln2_g[l]), row(ln2_b[l]))

    return xs.reshape(1, SEQ, D_MODEL)
```

```python
import functools

import jax
import jax.numpy as jnp
from jax import lax
from jax.experimental import pallas as pl
from jax.experimental.pallas import tpu as pltpu

F32 = jnp.float32
BF16 = jnp.bfloat16

D_MODEL = 2048
SEQ = 8192
DEPTH = 2
CHUNK = 64
N_HEADS = 16
QK_NOPE_DIM = 128
QK_ROPE_DIM = 64
V_HEAD_DIM = 128
QK_HEAD_DIM = QK_NOPE_DIM + QK_ROPE_DIM
Q_LORA = 512
KV_LORA = 512
ROPE_THETA = 10000.0
CONV_WIDTH = 31
D_FF_DENSE = 5632
N_EXPERTS = 8
D_FF_EXPERT = 7168
DEEPNORM_ALPHA = (2.0 * DEPTH) ** 0.25
LN_EPS = 1e-5
RMS_EPS = 1e-6
LATENT_COLS = Q_LORA + KV_LORA + 2 * QK_ROPE_DIM
CONV_HALO = 32
LANES = 128
SUBLANES = 8
EXPERT_ROW_TILE = 512

VMEM_LIMIT = 48 * 1024 * 1024
EXPERT_VMEM_LIMIT = 58 * 1024 * 1024


def _params(*sem):
    return pltpu.CompilerParams(dimension_semantics=sem, vmem_limit_bytes=VMEM_LIMIT)


def _layer_norm_rows(x):
    mu = jnp.mean(x, axis=-1, keepdims=True)
    xc = x - mu
    var = jnp.mean(xc * xc, axis=-1, keepdims=True)
    return xc * lax.rsqrt(var + LN_EPS)


def _sigmoid(x):
    return 1.0 / (1.0 + jnp.exp(-x))


ADA_STREAMS = 4


def _ada_kernel(c_ref, *rest, tk):
    w_refs, (b_ref, o_ref) = rest[:ADA_STREAMS], rest[ADA_STREAMS:]
    k = pl.program_id(1)

    @pl.when(k == 0)
    def _():
        o_ref[0] = b_ref[0]

    part = None
    for q, w_ref in enumerate(w_refs):
        c = c_ref[pl.ds(pl.multiple_of((k * ADA_STREAMS + q) * tk, tk), tk), :]
        term = jnp.sum(c * _sigmoid(c) * w_ref[0], axis=0, keepdims=True)
        part = term if part is None else part + term
    o_ref[0] += part


def _ada(c_col, w_ada, b_ada, tk=64):
    depth, d, n = w_ada.shape
    w_specs = [pl.BlockSpec((1, tk, n), functools.partial(lambda l, k, q: (l, k * ADA_STREAMS + q, 0), q=q))
               for q in range(ADA_STREAMS)]
    return pl.pallas_call(
        functools.partial(_ada_kernel, tk=tk),
        grid=(depth, d // (tk * ADA_STREAMS)),
        in_specs=[pl.BlockSpec((d, 1), lambda l, k: (0, 0))] + w_specs
        + [pl.BlockSpec((1, 1, n), lambda l, k: (l, 0, 0))],
        out_specs=pl.BlockSpec((1, 1, n), lambda l, k: (l, 0, 0)),
        out_shape=jax.ShapeDtypeStruct((depth, 1, n), F32),
        compiler_params=_params("parallel", "arbitrary"),
        name="ada_mod",
    )(c_col, *([w_ada] * ADA_STREAMS), b_ada.reshape(depth, 1, n))


def _rope_kernel(pos_col_ref, pos_row_ref, inv_row_ref, inv_col_ref, cos_ref, sin_ref, cos_t_ref, sin_t_ref):
    ang = pos_col_ref[...].astype(F32) * inv_row_ref[...]
    cos_ref[...] = jnp.cos(ang)
    sin_ref[...] = jnp.sin(ang)
    ang_t = inv_col_ref[...] * pos_row_ref[...].astype(F32)
    cos_t_ref[...] = jnp.cos(ang_t)
    sin_t_ref[...] = jnp.sin(ang_t)


def _rope_tables(positions, inv_freq2, ts=512):
    s = positions.shape[0]
    r = inv_freq2.shape[0]
    return pl.pallas_call(
        _rope_kernel,
        grid=(s // ts,),
        in_specs=[
            pl.BlockSpec((ts, 1), lambda i: (i, 0)),
            pl.BlockSpec((1, ts), lambda i: (0, i)),
            pl.BlockSpec((1, r), lambda i: (0, 0)),
            pl.BlockSpec((r, 1), lambda i: (0, 0)),
        ],
        out_specs=[pl.BlockSpec((ts, r), lambda i: (i, 0))] * 2 + [pl.BlockSpec((r, ts), lambda i: (0, i))] * 2,
        out_shape=[jax.ShapeDtypeStruct((s, r), F32)] * 2 + [jax.ShapeDtypeStruct((r, s), F32)] * 2,
        compiler_params=_params("parallel"),
        name="rope_tables",
    )(positions.reshape(s, 1), positions.reshape(1, s), inv_freq2.reshape(1, r), inv_freq2.reshape(r, 1))


def _lnmod_mm_kernel(x_ref, sc_ref, sh_ref, *rest, mode):
    if mode in ("glu", "swiglu"):
        wa_ref, wb_ref, o_ref, h_ref = rest
    else:
        wa_ref, o_ref, h_ref = rest
        wb_ref = None

    @pl.when(pl.program_id(1) == 0)
    def _():
        h = _layer_norm_rows(x_ref[...]) * (1.0 + sc_ref[...]) + sh_ref[...]
        h_ref[...] = h.astype(BF16)

    h = h_ref[...]
    a = jnp.dot(h, wa_ref[...], preferred_element_type=F32)
    if mode == "plain":
        o = a
    elif mode == "sigmoid":
        o = _sigmoid(a)
    else:
        b = jnp.dot(h, wb_ref[...], preferred_element_type=F32)
        if mode == "glu":
            o = a * _sigmoid(b)
        else:
            o = a * _sigmoid(a) * b
    o_ref[...] = o.astype(o_ref.dtype)


def _lnmod_mm(x, sc, sh, wa, wb=None, *, mode, out_dtype, tm=1024, tn=512, name):
    s, d = x.shape
    n = wa.shape[1]
    dual = wb is not None
    w_spec = pl.BlockSpec((d, tn), lambda i, j: (0, j))
    vec_spec = pl.BlockSpec((1, d), lambda i, j: (0, 0))
    in_specs = [pl.BlockSpec((tm, d), lambda i, j: (i, 0)), vec_spec, vec_spec, w_spec]
    args = [x, sc, sh, wa]
    if dual:
        in_specs.append(w_spec)
        args.append(wb)
    return pl.pallas_call(
        functools.partial(_lnmod_mm_kernel, mode=mode),
        grid=(s // tm, n // tn),
        in_specs=in_specs,
        out_specs=pl.BlockSpec((tm, tn), lambda i, j: (i, j)),
        out_shape=jax.ShapeDtypeStruct((s, n), out_dtype),
        scratch_shapes=[pltpu.VMEM((tm, d), BF16)],
        compiler_params=_params("parallel", "arbitrary"),
        name=name,
    )(*args)


def _conv_pw_kernel(prev_ref, cur_ref, wdw_ref, bdw_ref, lng_ref, lnb_ref, wpw_ref, o_ref,
                    ext_ref, conv_ref, act_ref, *, tm, cw, rh):
    i = pl.program_id(0)

    @pl.when(pl.program_id(1) == 0)
    def _():
        d = cur_ref.shape[1]
        halo = prev_ref[...].astype(F32)
        ext_ref[0:CONV_HALO, :] = jnp.where(i > 0, halo, 0.0)
        ext_ref[CONV_HALO:CONV_HALO + tm, :] = cur_ref[...].astype(F32)
        first = CONV_HALO - (CONV_WIDTH - 1)

        def chan_chunk(ci, carry):
            cs = pl.ds(pl.multiple_of(ci * cw, cw), cw)
            for r0 in range(0, tm, rh):
                partial = []
                for res in range(SUBLANES):
                    taps = [k for k in range(CONV_WIDTH) if (first + k) % SUBLANES == res]
                    lo = min(first + k for k in taps) - res
                    hi = max(first + k for k in taps) - res
                    shifted = ext_ref[r0 + lo + res:r0 + hi + res + rh, cs]
                    acc = None
                    for k in taps:
                        off = first + k - res - lo
                        prod = wdw_ref[k:k + 1, cs] * shifted[off:off + rh]
                        acc = prod if acc is None else acc + prod
                    partial.append(acc)
                while len(partial) > 1:
                    partial = [a + b for a, b in zip(partial[0::2], partial[1::2])]
                conv_ref[r0:r0 + rh, cs] = partial[0]
            return carry

        lax.fori_loop(0, d // cw, chan_chunk, 0)
        u = conv_ref[...] + bdw_ref[...]
        u = _layer_norm_rows(u) * lng_ref[...] + lnb_ref[...]
        act_ref[...] = (u * _sigmoid(u)).astype(BF16)

    o_ref[...] = jnp.dot(act_ref[...], wpw_ref[...], preferred_element_type=F32).astype(o_ref.dtype)


def _conv_pw(u, w_dw, b_dw, ln_g, ln_b, w_pw, *, tm=256, tn=2048, cw=256, rh=64):
    s, d = u.shape
    n = w_pw.shape[1]
    ratio = tm // CONV_HALO
    vec = pl.BlockSpec((1, d), lambda i, j: (0, 0))
    return pl.pallas_call(
        functools.partial(_conv_pw_kernel, tm=tm, cw=cw, rh=rh),
        grid=(s // tm, n // tn),
        in_specs=[
            pl.BlockSpec((CONV_HALO, d), lambda i, j: (jnp.maximum(i * ratio - 1, 0), 0)),
            pl.BlockSpec((tm, d), lambda i, j: (i, 0)),
            pl.BlockSpec((CONV_WIDTH, d), lambda i, j: (0, 0)),
            vec, vec, vec,
            pl.BlockSpec((d, tn), lambda i, j: (0, j)),
        ],
        out_specs=pl.BlockSpec((tm, tn), lambda i, j: (i, j)),
        out_shape=jax.ShapeDtypeStruct((s, n), BF16),
        scratch_shapes=[
            pltpu.VMEM((CONV_HALO + tm, d), F32),
            pltpu.VMEM((tm, d), F32),
            pltpu.VMEM((tm, d), BF16),
        ],
        compiler_params=_params("parallel", "arbitrary"),
        name="conv_pw",
    )(u, u, w_dw, b_dw, ln_g, ln_b, w_pw)


def _rms_rows(x, g):
    return x * lax.rsqrt(jnp.mean(x * x, axis=-1, keepdims=True) + RMS_EPS) * g


_NT = (((1,), (1,)), ((), ()))


def _qproj_kernel(cq_ref, g_ref, wt_ref, wrot_t_ref, cos_t_ref, sin_t_ref, o_ref, *, scale):
    cn = _rms_rows(cq_ref[...], g_ref[...]).astype(BF16)
    main_t = lax.dot_general(wt_ref[...], cn, _NT, preferred_element_type=F32)
    rot_t = lax.dot_general(wrot_t_ref[...], cn, _NT, preferred_element_type=F32)
    cos_t = cos_t_ref[...]
    sin_t = sin_t_ref[...]
    n_nope = N_HEADS * QK_NOPE_DIM
    for h in range(N_HEADS):
        nope = main_t[h * QK_NOPE_DIM:(h + 1) * QK_NOPE_DIM]
        r0 = h * QK_ROPE_DIM
        rope = main_t[n_nope + r0:n_nope + r0 + QK_ROPE_DIM] * cos_t + rot_t[r0:r0 + QK_ROPE_DIM] * sin_t
        o_ref[h, 0:QK_NOPE_DIM, :] = (nope * scale).astype(BF16)
        o_ref[h, QK_NOPE_DIM:QK_HEAD_DIM, :] = (rope * scale).astype(BF16)


def _qproj(lat, g_q, wq_t, wq_rot_t, cos_t, sin_t, *, tm=256):
    s = lat.shape[0]
    scale = QK_HEAD_DIM ** -0.5 * 1.4426950408889634
    return pl.pallas_call(
        functools.partial(_qproj_kernel, scale=scale),
        grid=(s // tm,),
        in_specs=[
            pl.BlockSpec((tm, Q_LORA), lambda i: (i, 0)),
            pl.BlockSpec((1, Q_LORA), lambda i: (0, 0)),
            pl.BlockSpec(wq_t.shape, lambda i: (0, 0)),
            pl.BlockSpec(wq_rot_t.shape, lambda i: (0, 0)),
            pl.BlockSpec((QK_ROPE_DIM, tm), lambda i: (0, i)),
            pl.BlockSpec((QK_ROPE_DIM, tm), lambda i: (0, i)),
        ],
        out_specs=pl.BlockSpec((N_HEADS, QK_HEAD_DIM, tm), lambda i: (0, 0, i)),
        out_shape=jax.ShapeDtypeStruct((N_HEADS, QK_HEAD_DIM, s), BF16),
        compiler_params=_params("parallel"),
        name="q_proj",
    )(lat, g_q, wq_t, wq_rot_t, cos_t, sin_t)


def _kvproj_kernel(ckv_ref, kr_ref, g_ref, wk_ref, wv_t_ref, cos_ref, sin_ref, k_ref, vt_ref):
    cn = _rms_rows(ckv_ref[...], g_ref[...]).astype(BF16)
    kr = kr_ref[:, 0:QK_ROPE_DIM]
    kr_rot = kr_ref[:, QK_ROPE_DIM:2 * QK_ROPE_DIM]
    krope = (kr * cos_ref[...] + kr_rot * sin_ref[...]).astype(BF16)
    k_all = jnp.dot(cn, wk_ref[...], preferred_element_type=F32)
    v_t = lax.dot_general(wv_t_ref[...], cn, _NT, preferred_element_type=F32)
    for h in range(N_HEADS):
        k_ref[h, :, 0:QK_NOPE_DIM] = k_all[:, h * QK_NOPE_DIM:(h + 1) * QK_NOPE_DIM].astype(BF16)
        k_ref[h, :, QK_NOPE_DIM:QK_HEAD_DIM] = krope
        vt_ref[h] = v_t[h * V_HEAD_DIM:(h + 1) * V_HEAD_DIM].astype(BF16)


def _kvproj(lat, g_kv, wk, wv_t, cos, sin, *, tm=256):
    s = lat.shape[0]
    kr_block = (Q_LORA + KV_LORA) // (2 * QK_ROPE_DIM)
    return pl.pallas_call(
        _kvproj_kernel,
        grid=(s // tm,),
        in_specs=[
            pl.BlockSpec((tm, KV_LORA), lambda i: (i, Q_LORA // KV_LORA)),
            pl.BlockSpec((tm, 2 * QK_ROPE_DIM), lambda i: (i, kr_block)),
            pl.BlockSpec((1, KV_LORA), lambda i: (0, 0)),
            pl.BlockSpec(wk.shape, lambda i: (0, 0)),
            pl.BlockSpec(wv_t.shape, lambda i: (0, 0)),
            pl.BlockSpec((tm, QK_ROPE_DIM), lambda i: (i, 0)),
            pl.BlockSpec((tm, QK_ROPE_DIM), lambda i: (i, 0)),
        ],
        out_specs=[
            pl.BlockSpec((N_HEADS, tm, QK_HEAD_DIM), lambda i: (0, i, 0)),
            pl.BlockSpec((N_HEADS, V_HEAD_DIM, tm), lambda i: (0, 0, i)),
        ],
        out_shape=[
            jax.ShapeDtypeStruct((N_HEADS, s, QK_HEAD_DIM), BF16),
            jax.ShapeDtypeStruct((N_HEADS, V_HEAD_DIM, s), BF16),
        ],
        compiler_params=_params("parallel"),
        name="kv_proj",
    )(lat, lat, g_kv, wk, wv_t, cos, sin)


def _flash_kernel(qt_ref, k_ref, vt_ref, o_ref, sa_ref, sb_ref, m_ref, l_ref, acc_ref, *, tq, tk):
    qi = pl.program_id(1)
    qt = qt_ref[0]

    def scores(j, s_ref):
        start = pl.multiple_of(j * tk, tk)
        s_ref[...] = jnp.dot(k_ref[0, pl.ds(start, tk), :], qt, preferred_element_type=F32)

    def consume(j, s_ref, mask):
        s = s_ref[...]
        if mask is not None:
            s = jnp.where(mask, s, -jnp.inf)
        m_prev = m_ref[...]
        m_new = jnp.maximum(m_prev, jnp.max(s, axis=0, keepdims=True))
        alpha = jnp.exp2(m_prev - m_new)
        p = jnp.exp2(s - m_new)
        l_ref[...] = alpha * l_ref[...] + jnp.sum(p, axis=0, keepdims=True)
        start = pl.multiple_of(j * tk, tk)
        pv = jnp.dot(vt_ref[0, :, pl.ds(start, tk)], p.astype(BF16), preferred_element_type=F32)
        acc_ref[...] = alpha * acc_ref[...] + pv
        m_ref[...] = m_new

    m_ref[...] = jnp.full(m_ref.shape, -jnp.inf, F32)
    l_ref[...] = jnp.zeros(l_ref.shape, F32)
    acc_ref[...] = jnp.zeros(acc_ref.shape, F32)

    scores(0, sa_ref)

    def pair(tt, carry):
        j = 2 * tt
        scores(j + 1, sb_ref)
        consume(j, sa_ref, None)
        scores(j + 2, sa_ref)
        consume(j + 1, sb_ref, None)
        return carry

    lax.fori_loop(0, qi, pair, 0)

    key_chunk = lax.broadcasted_iota(jnp.int32, (tk, tq), 0) // CHUNK
    query_chunk = lax.broadcasted_iota(jnp.int32, (tk, tq), 1) // CHUNK
    j_diag = 2 * qi
    scores(j_diag + 1, sb_ref)
    consume(j_diag, sa_ref, key_chunk <= query_chunk)
    consume(j_diag + 1, sb_ref, key_chunk + tk // CHUNK <= query_chunk)

    out_t = acc_ref[...] / l_ref[...]
    o_ref[...] = out_t.T.astype(o_ref.dtype)


def _flash(q_t, k, v_t, *, tq=1024, tk=512):
    assert tq == 2 * tk
    h, _, s = q_t.shape
    return pl.pallas_call(
        functools.partial(_flash_kernel, tq=tq, tk=tk),
        grid=(h, s // tq),
        in_specs=[
            pl.BlockSpec((1, QK_HEAD_DIM, tq), lambda hh, i: (hh, 0, i)),
            pl.BlockSpec((1, s, QK_HEAD_DIM), lambda hh, i: (hh, 0, 0)),
            pl.BlockSpec((1, V_HEAD_DIM, s), lambda hh, i: (hh, 0, 0)),
        ],
        out_specs=pl.BlockSpec((tq, V_HEAD_DIM), lambda hh, i: (i, hh)),
        out_shape=jax.ShapeDtypeStruct((s, h * V_HEAD_DIM), BF16),
        scratch_shapes=[
            pltpu.VMEM((tk, tq), F32),
            pltpu.VMEM((tk, tq), F32),
            pltpu.VMEM((1, tq), F32),
            pltpu.VMEM((1, tq), F32),
            pltpu.VMEM((V_HEAD_DIM, tq), F32),
        ],
        compiler_params=_params("parallel", "arbitrary"),
        name="flash_attn",
    )(q_t, k, v_t)


def _wo_mix_kernel(o_ref, w_ref, gc_ref, gm_ref, yc_ref, out_ref):
    y_mla = jnp.dot(o_ref[...], w_ref[...], preferred_element_type=F32)
    mixed = gc_ref[...].astype(F32) * yc_ref[...].astype(F32) + gm_ref[...].astype(F32) * y_mla
    out_ref[...] = mixed.astype(out_ref.dtype)


def _wo_mix(o, w_o, gates, y_conv, *, tm=1024, tn=512):
    s, d = o.shape
    n = w_o.shape[1]
    nb = n // tn
    return pl.pallas_call(
        _wo_mix_kernel,
        grid=(s // tm, nb),
        in_specs=[
            pl.BlockSpec((tm, d), lambda i, j: (i, 0)),
            pl.BlockSpec((d, tn), lambda i, j: (0, j)),
            pl.BlockSpec((tm, tn), lambda i, j: (i, j)),
            pl.BlockSpec((tm, tn), lambda i, j: (i, j + nb)),
            pl.BlockSpec((tm, tn), lambda i, j: (i, j)),
        ],
        out_specs=pl.BlockSpec((tm, tn), lambda i, j: (i, j)),
        out_shape=jax.ShapeDtypeStruct((s, n), BF16),
        compiler_params=_params("parallel", "parallel"),
        name="wo_mix",
    )(o, w_o, gates, gates, y_conv)


def _resid_ln(x, f, gate, lng, lnb):
    y = DEEPNORM_ALPHA * x + (1.0 + gate) * f
    return _layer_norm_rows(y) * lng + lnb


def _mm_resid_ln_kernel(a_ref, w_ref, x_ref, gate_ref, lng_ref, lnb_ref, o_ref, acc_ref):
    kk = pl.program_id(1)

    @pl.when(kk == 0)
    def _():
        acc_ref[...] = jnp.zeros(acc_ref.shape, F32)

    acc_ref[...] += jnp.dot(a_ref[...], w_ref[...], preferred_element_type=F32)

    @pl.when(kk == pl.num_programs(1) - 1)
    def _():
        o_ref[...] = _resid_ln(x_ref[...], acc_ref[...], gate_ref[...], lng_ref[...], lnb_ref[...])


def _mm_resid_ln(a, w, x, gate, lng, lnb, *, tm=512, tk, name):
    s, kdim = a.shape
    d = w.shape[1]
    vec = pl.BlockSpec((1, d), lambda i, k: (0, 0))
    return pl.pallas_call(
        _mm_resid_ln_kernel,
        grid=(s // tm, kdim // tk),
        in_specs=[
            pl.BlockSpec((tm, tk), lambda i, k: (i, k)),
            pl.BlockSpec((tk, d), lambda i, k: (k, 0)),
            pl.BlockSpec((tm, d), lambda i, k: (i, 0)),
            vec, vec, vec,
        ],
        out_specs=pl.BlockSpec((tm, d), lambda i, k: (i, 0)),
        out_shape=jax.ShapeDtypeStruct((s, d), F32),
        scratch_shapes=[pltpu.VMEM((tm, d), F32)],
        compiler_params=_params("parallel", "arbitrary"),
        name=name,
    )(a, w, x, gate, lng, lnb)


def _router_kernel(x_ref, sc_ref, sh_ref, wr_ref, h_ref, comb_ref, sel_ref):
    h = _layer_norm_rows(x_ref[...]) * (1.0 + sc_ref[...]) + sh_ref[...]
    h_ref[...] = h
    logits = jnp.dot(h, wr_ref[...], preferred_element_type=F32, precision=lax.Precision.HIGHEST)
    lane = lax.broadcasted_iota(jnp.int32, logits.shape, 1)
    neg = -jnp.inf
    l1 = jnp.where(lane < N_EXPERTS, logits, neg)
    v1 = jnp.max(l1, axis=1, keepdims=True)
    i1 = jnp.min(jnp.where(l1 == v1, lane, LANES), axis=1, keepdims=True)
    l2 = jnp.where(lane == i1, neg, l1)
    v2 = jnp.max(l2, axis=1, keepdims=True)
    i2 = jnp.min(jnp.where(l2 == v2, lane, LANES), axis=1, keepdims=True)
    e2 = jnp.exp(v2 - v1)
    w1 = 1.0 / (1.0 + e2)
    w2 = e2 / (1.0 + e2)
    comb_ref[...] = jnp.where(lane == i1, w1, 0.0) + jnp.where(lane == i2, w2, 0.0)
    sel_ref[...] = jnp.where(lane == i1, 1.0, 0.0) + jnp.where(lane == i2, 2.0, 0.0)


def _router(x, sc, sh, wr_pad, *, tm=256):
    s, d = x.shape
    vec = pl.BlockSpec((1, d), lambda i: (0, 0))
    lane_blk = pl.BlockSpec((tm, LANES), lambda i: (i, 0))
    return pl.pallas_call(
        _router_kernel,
        grid=(s // tm,),
        in_specs=[pl.BlockSpec((tm, d), lambda i: (i, 0)), vec, vec,
                  pl.BlockSpec((d, LANES), lambda i: (0, 0))],
        out_specs=[pl.BlockSpec((tm, d), lambda i: (i, 0)), lane_blk, lane_blk],
        out_shape=[jax.ShapeDtypeStruct((s, d), F32), jax.ShapeDtypeStruct((s, LANES), F32),
                   jax.ShapeDtypeStruct((s, LANES), F32)],
        compiler_params=_params("parallel"),
        name="router",
    )(x, sc, sh, wr_pad)


def _routing_plan(comb, sel, tm, n_tiles):
    sel8 = sel[:, :N_EXPERTS]
    comb8 = comb[:, :N_EXPERTS]
    chosen = (sel8 > 0).astype(jnp.int32)
    rank = jnp.cumsum(chosen, axis=0) - chosen
    counts = jnp.sum(chosen, axis=0)
    tiles_e = (counts + tm - 1) // tm
    tile_end = jnp.cumsum(tiles_e)
    row_start = (tile_end - tiles_e) * tm
    dest = row_start[None, :] + rank
    pos0 = jnp.sum(jnp.where(sel8 == 1.0, dest, 0), axis=1).astype(jnp.int32)
    pos1 = jnp.sum(jnp.where(sel8 == 2.0, dest, 0), axis=1).astype(jnp.int32)
    w0 = jnp.sum(jnp.where(sel8 == 1.0, comb8, 0.0), axis=1, keepdims=True)
    w1 = jnp.sum(jnp.where(sel8 == 2.0, comb8, 0.0), axis=1, keepdims=True)
    n_valid = tile_end[-1]
    tile_id = jnp.minimum(jnp.arange(n_tiles, dtype=jnp.int32), n_valid - 1)
    tile_expert = jnp.sum((tile_id[:, None] >= tile_end[None, :]).astype(jnp.int32), axis=1)
    meta = jnp.concatenate([tile_expert, n_valid[None]]).astype(jnp.int32)
    tok = jnp.arange(sel.shape[0], dtype=jnp.int32)
    src_token = jnp.zeros((n_tiles * tm,), jnp.int32).at[jnp.concatenate([pos0, pos1])].set(
        jnp.concatenate([tok, tok]), unique_indices=True)
    return pos0, pos1, w0, w1, meta, src_token


def _row_copy(src_hbm, dst_hbm, src_row, dst_row, sem):
    return pltpu.make_async_copy(src_hbm.at[pl.ds(src_row, 1)], dst_hbm.at[pl.ds(dst_row, 1)], sem)


def _dispatch_kernel(meta_ref, src_ref, src_next_ref, h_hbm, o_ref, buf, sem, *, tb, n_tiles):
    i = pl.program_id(0)
    n_used = meta_ref[n_tiles]
    slot = i % 2

    def start_gather(idx_ref, dst_slot):
        def issue(r, carry):
            _row_copy(h_hbm, buf.at[dst_slot], idx_ref[0, 0, r], r, sem.at[dst_slot]).start()
            return carry

        lax.fori_loop(0, tb, issue, 0)

    @pl.when(jnp.logical_and(i == 0, n_used > 0))
    def _():
        start_gather(src_ref, 0)

    @pl.when(i + 1 < n_used)
    def _():
        start_gather(src_next_ref, 1 - slot)

    @pl.when(i < n_used)
    def _():
        def drain(r, carry):
            _row_copy(h_hbm, buf.at[slot], 0, 0, sem.at[slot]).wait()
            return carry

        lax.fori_loop(0, tb, drain, 0)
        o_ref[...] = buf[slot].astype(o_ref.dtype)

    @pl.when(i >= n_used)
    def _():
        o_ref[...] = jnp.zeros(o_ref.shape, o_ref.dtype)


def _dispatch(meta, src_token, h, *, tb):
    n_rows = src_token.shape[0]
    d = h.shape[1]
    n_tiles = n_rows // tb
    return pl.pallas_call(
        functools.partial(_dispatch_kernel, tb=tb, n_tiles=n_tiles),
        grid_spec=pltpu.PrefetchScalarGridSpec(
            num_scalar_prefetch=1,
            grid=(n_tiles,),
            in_specs=[
                pl.BlockSpec((1, 1, tb), lambda i, m: (i, 0, 0), memory_space=pltpu.SMEM),
                pl.BlockSpec((1, 1, tb), lambda i, m: (jnp.minimum(i + 1, n_tiles - 1), 0, 0),
                             memory_space=pltpu.SMEM),
                pl.BlockSpec(memory_space=pl.ANY),
            ],
            out_specs=pl.BlockSpec((tb, d), lambda i, m: (i, 0)),
            scratch_shapes=[pltpu.VMEM((2, tb, d), F32), pltpu.SemaphoreType.DMA((2,))],
        ),
        out_shape=jax.ShapeDtypeStruct((n_rows, d), BF16),
        compiler_params=_params("arbitrary"),
        name="moe_dispatch",
    )(meta, src_token.reshape(n_tiles, 1, tb), src_token.reshape(n_tiles, 1, tb), h)


def _expert_swiglu_kernel(meta_ref, x_ref, wg_ref, wu_ref, o_ref, wg_bf, wu_bf, *, n_tiles):
    i = pl.program_id(1)

    @pl.when(jnp.logical_or(i == 0, meta_ref[i] != meta_ref[jnp.maximum(i - 1, 0)]))
    def _():
        wg_bf[...] = wg_ref[0, 0].astype(BF16)
        wu_bf[...] = wu_ref[0, 0].astype(BF16)

    @pl.when(i < meta_ref[n_tiles])
    def _():
        x = x_ref[...]
        a = jnp.dot(x, wg_bf[...], preferred_element_type=F32)
        b = jnp.dot(x, wu_bf[...], preferred_element_type=F32)
        o_ref[...] = (a * _sigmoid(a) * b).astype(o_ref.dtype)

    @pl.when(i >= meta_ref[n_tiles])
    def _():
        o_ref[...] = jnp.zeros(o_ref.shape, o_ref.dtype)


def _expert_swiglu(meta, xs, wg, wu, *, layer, tm, tn=1024):
    r, d = xs.shape
    f = wg.shape[3]
    n_tiles = r // tm
    w_spec = pl.BlockSpec((1, 1, d, tn), lambda j, i, m: (layer, m[i], 0, j))

    def row_tile(i, m):
        return jnp.minimum(i, m[n_tiles] - 1)

    return pl.pallas_call(
        functools.partial(_expert_swiglu_kernel, n_tiles=n_tiles),
        grid_spec=pltpu.PrefetchScalarGridSpec(
            num_scalar_prefetch=1,
            grid=(f // tn, n_tiles),
            in_specs=[
                pl.BlockSpec((tm, d), lambda j, i, m: (row_tile(i, m), 0)),
                w_spec,
                w_spec,
            ],
            out_specs=pl.BlockSpec((tm, tn), lambda j, i, m: (i, j)),
            scratch_shapes=[pltpu.VMEM((d, tn), BF16), pltpu.VMEM((d, tn), BF16)],
        ),
        out_shape=jax.ShapeDtypeStruct((r, f), BF16),
        compiler_params=pltpu.CompilerParams(dimension_semantics=("arbitrary", "arbitrary"),
                                             vmem_limit_bytes=EXPERT_VMEM_LIMIT),
        name="expert_swiglu",
    )(meta, xs, wg, wu)


def _expert_down_kernel(meta_ref, a_ref, w_ref, o_ref, acc_ref, *, n_tiles):
    i = pl.program_id(0)
    kk = pl.program_id(1)

    @pl.when(i < meta_ref[n_tiles])
    def _():
        @pl.when(kk == 0)
        def _():
            acc_ref[...] = jnp.zeros(acc_ref.shape, F32)

        acc_ref[...] += jnp.dot(a_ref[...], w_ref[0], preferred_element_type=F32)

        @pl.when(kk == pl.num_programs(1) - 1)
        def _():
            o_ref[...] = acc_ref[...]

    @pl.when(jnp.logical_and(i >= meta_ref[n_tiles], kk == 0))
    def _():
        o_ref[...] = jnp.zeros(o_ref.shape, o_ref.dtype)


def _expert_down(meta, act, wd, *, tm, tk=1024):
    r, f = act.shape
    d = wd.shape[2]
    n_tiles = r // tm
    nk = f // tk

    def used(i, m):
        return i < m[n_tiles]

    def row_tile(i, m):
        return jnp.minimum(i, m[n_tiles] - 1)

    def k_tile(i, k, m):
        return jnp.where(used(i, m), k, nk - 1)

    return pl.pallas_call(
        functools.partial(_expert_down_kernel, n_tiles=n_tiles),
        grid_spec=pltpu.PrefetchScalarGridSpec(
            num_scalar_prefetch=1,
            grid=(n_tiles, nk),
            in_specs=[
                pl.BlockSpec((tm, tk), lambda i, k, m: (row_tile(i, m), k_tile(i, k, m))),
                pl.BlockSpec((1, tk, d), lambda i, k, m: (m[i], k_tile(i, k, m), 0)),
            ],
            out_specs=pl.BlockSpec((tm, d), lambda i, k, m: (i, 0)),
            scratch_shapes=[pltpu.VMEM((tm, d), F32)],
        ),
        out_shape=jax.ShapeDtypeStruct((r, d), F32),
        compiler_params=_params("arbitrary", "arbitrary"),
        name="expert_down",
    )(meta, act, wd)


def _combine_ln_kernel(pos0_ref, pos1_ref, pos0_next_ref, pos1_next_ref, ys_hbm, w0_ref, w1_ref, x_ref,
                       gate_ref, lng_ref, lnb_ref, o_ref, a_buf, b_buf, sem, *, tb):
    i = pl.program_id(0)
    slot = i % 2

    def start_gather(p0_ref, p1_ref, dst_slot):
        def issue(r, carry):
            _row_copy(ys_hbm, a_buf.at[dst_slot], p0_ref[0, 0, r], r, sem.at[dst_slot]).start()
            _row_copy(ys_hbm, b_buf.at[dst_slot], p1_ref[0, 0, r], r, sem.at[dst_slot]).start()
            return carry

        lax.fori_loop(0, tb, issue, 0)

    @pl.when(i == 0)
    def _():
        start_gather(pos0_ref, pos1_ref, 0)

    @pl.when(i + 1 < pl.num_programs(0))
    def _():
        start_gather(pos0_next_ref, pos1_next_ref, 1 - slot)

    def drain(r, carry):
        _row_copy(ys_hbm, a_buf.at[slot], 0, 0, sem.at[slot]).wait()
        _row_copy(ys_hbm, b_buf.at[slot], 0, 0, sem.at[slot]).wait()
        return carry

    lax.fori_loop(0, tb, drain, 0)
    f = w0_ref[...] * a_buf[slot] + w1_ref[...] * b_buf[slot]
    o_ref[...] = _resid_ln(x_ref[...], f, gate_ref[...], lng_ref[...], lnb_ref[...])


def _combine_ln(ys, pos0, pos1, w0, w1, x, gate, lng, lnb, *, tb=256):
    s, d = x.shape
    n_blk = s // tb
    idx_blk = pl.BlockSpec((1, 1, tb), lambda i: (i, 0, 0), memory_space=pltpu.SMEM)
    idx_next = pl.BlockSpec((1, 1, tb), lambda i: (jnp.minimum(i + 1, n_blk - 1), 0, 0), memory_space=pltpu.SMEM)
    vec = pl.BlockSpec((1, d), lambda i: (0, 0))
    col = pl.BlockSpec((tb, 1), lambda i: (i, 0))
    blk = pl.BlockSpec((tb, d), lambda i: (i, 0))
    p0 = pos0.reshape(n_blk, 1, tb)
    p1 = pos1.reshape(n_blk, 1, tb)
    return pl.pallas_call(
        functools.partial(_combine_ln_kernel, tb=tb),
        grid=(n_blk,),
        in_specs=[idx_blk, idx_blk, idx_next, idx_next, pl.BlockSpec(memory_space=pl.ANY),
                  col, col, blk, vec, vec, vec],
        out_specs=blk,
        out_shape=jax.ShapeDtypeStruct((s, d), F32),
        scratch_shapes=[pltpu.VMEM((2, tb, d), F32), pltpu.VMEM((2, tb, d), F32), pltpu.SemaphoreType.DMA((2,))],
        compiler_params=_params("arbitrary"),
        name="moe_combine_ln",
    )(p0, p1, p0, p1, ys, w0, w1, x, gate, lng, lnb)


def _rotate_half_cols(w):
    half = w.shape[-1] // 2
    return jnp.concatenate([-w[..., half:], w[..., :half]], axis=-1)


def kernel(x, c, positions, w_ada, b_ada, w_in, g_q, w_uq, g_kv, w_ukv, w_o, w_dw, b_dw, conv_ln_g, conv_ln_b, w_pw, w_out, ln1_g, ln1_b, ln2_g, ln2_b, w_ff_gate, w_ff_up, w_ff_down, w_router, w_e_gate, w_e_up, w_e_down):
    assert x.shape == (1, SEQ, D_MODEL) and c.shape == (1, D_MODEL)
    d = D_MODEL
    xs = x.reshape(SEQ, d)

    mod = _ada(c.reshape(d, 1), w_ada, b_ada).reshape(DEPTH, 6, 1, d)

    inv_freq = ROPE_THETA ** (-jnp.arange(0, QK_ROPE_DIM, 2, dtype=F32) / QK_ROPE_DIM)
    inv_freq2 = jnp.concatenate([inv_freq, inv_freq])
    cos, sin, cos_t, sin_t = _rope_tables(positions.reshape(SEQ), inv_freq2)

    row = lambda v: v.reshape(1, -1)

    for l in range(DEPTH):
        sh1, sc1, g1, sh2, sc2, g2 = (mod[l, t] for t in range(6))
        wl = w_in[l]
        o_kr = Q_LORA + KV_LORA
        o_val = o_kr + QK_ROPE_DIM
        w_kr = wl[:, o_kr:o_val]
        w_lat = jnp.concatenate([wl[:, :o_val], _rotate_half_cols(w_kr)], axis=1).astype(BF16)
        w_val = wl[:, o_val:o_val + d].astype(BF16)
        w_gate = wl[:, o_val + d:o_val + 2 * d].astype(BF16)
        w_brg = wl[:, o_val + 2 * d:].astype(BF16)

        lat = _lnmod_mm(xs, sc1, sh1, w_lat, mode="plain", out_dtype=F32, tn=LATENT_COLS // 3, name="in_latent")
        u = _lnmod_mm(xs, sc1, sh1, w_val, w_gate, mode="glu", out_dtype=BF16, name="in_glu")
        gates = _lnmod_mm(xs, sc1, sh1, w_brg, mode="sigmoid", out_dtype=BF16, name="in_gates")

        y_conv = _conv_pw(u, w_dw[l], row(b_dw[l]), row(conv_ln_g[l]), row(conv_ln_b[l]), w_pw[l].astype(BF16))

        wq = w_uq[l].reshape(Q_LORA, N_HEADS, QK_HEAD_DIM)
        wq_rope = wq[..., QK_NOPE_DIM:]
        wq_t = jnp.concatenate([wq[..., :QK_NOPE_DIM].reshape(Q_LORA, -1), wq_rope.reshape(Q_LORA, -1)], axis=1).T
        wq_rot_t = _rotate_half_cols(wq_rope).reshape(Q_LORA, -1).T
        wkv = w_ukv[l].reshape(KV_LORA, N_HEADS, QK_NOPE_DIM + V_HEAD_DIM)
        wk = wkv[..., :QK_NOPE_DIM].reshape(KV_LORA, -1)
        wv_t = wkv[..., QK_NOPE_DIM:].reshape(KV_LORA, -1).T
        q_t = _qproj(lat, row(g_q[l]), wq_t.astype(BF16), wq_rot_t.astype(BF16), cos_t, sin_t)
        k, v_t = _kvproj(lat, row(g_kv[l]), wk.astype(BF16), wv_t.astype(BF16), cos, sin)
        o = _flash(q_t, k, v_t)
        mixed = _wo_mix(o, w_o[l].astype(BF16), gates, y_conv)
        xs = _mm_resid_ln(mixed, w_out[l].astype(BF16), xs, g1, row(ln1_g[l]), row(ln1_b[l]), tk=d, name="out_proj_ln")

        i = l // 2
        if l % 2 == 0:
            act = _lnmod_mm(xs, sc2, sh2, w_ff_gate[i].astype(BF16), w_ff_up[i].astype(BF16),
                            mode="swiglu", out_dtype=BF16, name="ffn_swiglu")
            xs = _mm_resid_ln(act, w_ff_down[i].astype(BF16), xs, g2, row(ln2_g[l]), row(ln2_b[l]), tk=D_FF_DENSE // 4, name="ffn_down_ln")
        else:
            wr_pad = jnp.zeros((d, LANES), F32).at[:, :N_EXPERTS].set(w_router[i])
            h2, comb, sel = _router(xs, sc2, sh2, wr_pad)
            tm_e = EXPERT_ROW_TILE
            n_tiles = (2 * SEQ) // tm_e + N_EXPERTS
            pos0, pos1, w0, w1, meta, src_token = _routing_plan(comb, sel, tm_e, n_tiles)
            x_sorted = _dispatch(meta, src_token, h2, tb=tm_e)
            act = _expert_swiglu(meta, x_sorted, w_e_gate, w_e_up, layer=i, tm=tm_e)
            y_sorted = _expert_down(meta, act, w_e_down[i].astype(BF16), tm=tm_e)
            xs = _combine_ln(y_sorted, pos0, pos1, w0, w1, xs, g2, row(ln2_g[l]), row(ln2_b[l]))

    return xs.reshape(1, SEQ, D_MODEL)
```

```python
import functools

import jax
import jax.numpy as jnp
from jax import lax
from jax.experimental import pallas as pl
from jax.experimental.pallas import tpu as pltpu

F32 = jnp.float32
BF16 = jnp.bfloat16

D_MODEL = 2048
SEQ = 8192
DEPTH = 2
CHUNK = 64
N_HEADS = 16
QK_NOPE_DIM = 128
QK_ROPE_DIM = 64
V_HEAD_DIM = 128
QK_HEAD_DIM = QK_NOPE_DIM + QK_ROPE_DIM
Q_LORA = 512
KV_LORA = 512
ROPE_THETA = 10000.0
CONV_WIDTH = 31
D_FF_DENSE = 5632
N_EXPERTS = 8
D_FF_EXPERT = 7168
DEEPNORM_ALPHA = (2.0 * DEPTH) ** 0.25
LN_EPS = 1e-5
RMS_EPS = 1e-6
LATENT_COLS = Q_LORA + KV_LORA + 2 * QK_ROPE_DIM
CONV_HALO = 32
LANES = 128
SUBLANES = 8
EXPERT_ROW_TILE = 512

VMEM_LIMIT = 48 * 1024 * 1024
EXPERT_VMEM_LIMIT = 58 * 1024 * 1024


def _params(*sem):
    return pltpu.CompilerParams(dimension_semantics=sem, vmem_limit_bytes=VMEM_LIMIT)


def _layer_norm_rows(x):
    mu = jnp.mean(x, axis=-1, keepdims=True)
    xc = x - mu
    var = jnp.mean(xc * xc, axis=-1, keepdims=True)
    return xc * lax.rsqrt(var + LN_EPS)


def _sigmoid(x):
    return 1.0 / (1.0 + jnp.exp(-x))


ADA_STREAMS = 4


def _ada_kernel(c_ref, *rest, tk):
    w_refs, (b_ref, o_ref) = rest[:ADA_STREAMS], rest[ADA_STREAMS:]
    k = pl.program_id(1)

    @pl.when(k == 0)
    def _():
        o_ref[0] = b_ref[0]

    part = None
    for q, w_ref in enumerate(w_refs):
        c = c_ref[pl.ds(pl.multiple_of((k * ADA_STREAMS + q) * tk, tk), tk), :]
        term = jnp.sum(c * _sigmoid(c) * w_ref[0], axis=0, keepdims=True)
        part = term if part is None else part + term
    o_ref[0] += part


def _ada(c_col, w_ada, b_ada, tk=64):
    depth, d, n = w_ada.shape
    w_specs = [pl.BlockSpec((1, tk, n), functools.partial(lambda l, k, q: (l, k * ADA_STREAMS + q, 0), q=q))
               for q in range(ADA_STREAMS)]
    return pl.pallas_call(
        functools.partial(_ada_kernel, tk=tk),
        grid=(depth, d // (tk * ADA_STREAMS)),
        in_specs=[pl.BlockSpec((d, 1), lambda l, k: (0, 0))] + w_specs
        + [pl.BlockSpec((1, 1, n), lambda l, k: (l, 0, 0))],
        out_specs=pl.BlockSpec((1, 1, n), lambda l, k: (l, 0, 0)),
        out_shape=jax.ShapeDtypeStruct((depth, 1, n), F32),
        compiler_params=_params("parallel", "arbitrary"),
        name="ada_mod",
    )(c_col, *([w_ada] * ADA_STREAMS), b_ada.reshape(depth, 1, n))


def _rope_kernel(pos_col_ref, pos_row_ref, inv_row_ref, inv_col_ref, cos_ref, sin_ref, cos_t_ref, sin_t_ref):
    ang = pos_col_ref[...].astype(F32) * inv_row_ref[...]
    cos_ref[...] = jnp.cos(ang)
    sin_ref[...] = jnp.sin(ang)
    ang_t = inv_col_ref[...] * pos_row_ref[...].astype(F32)
    cos_t_ref[...] = jnp.cos(ang_t)
    sin_t_ref[...] = jnp.sin(ang_t)


def _rope_tables(positions, inv_freq2, ts=512):
    s = positions.shape[0]
    r = inv_freq2.shape[0]
    return pl.pallas_call(
        _rope_kernel,
        grid=(s // ts,),
        in_specs=[
            pl.BlockSpec((ts, 1), lambda i: (i, 0)),
            pl.BlockSpec((1, ts), lambda i: (0, i)),
            pl.BlockSpec((1, r), lambda i: (0, 0)),
            pl.BlockSpec((r, 1), lambda i: (0, 0)),
        ],
        out_specs=[pl.BlockSpec((ts, r), lambda i: (i, 0))] * 2 + [pl.BlockSpec((r, ts), lambda i: (0, i))] * 2,
        out_shape=[jax.ShapeDtypeStruct((s, r), F32)] * 2 + [jax.ShapeDtypeStruct((r, s), F32)] * 2,
        compiler_params=_params("parallel"),
        name="rope_tables",
    )(positions.reshape(s, 1), positions.reshape(1, s), inv_freq2.reshape(1, r), inv_freq2.reshape(r, 1))


def _ffn_swiglu_kernel(x_ref, sc_ref, sh_ref, wg_ref, wu_ref, o_ref, h_ref):
    @pl.when(pl.program_id(1) == 0)
    def _():
        h = _layer_norm_rows(x_ref[...]) * (1.0 + sc_ref[...]) + sh_ref[...]
        h_ref[...] = h.astype(BF16)

    h = h_ref[...]
    a = jnp.dot(h, wg_ref[...], preferred_element_type=F32)
    b = jnp.dot(h, wu_ref[...], preferred_element_type=F32)
    o_ref[...] = (a * _sigmoid(a) * b).astype(o_ref.dtype)


def _ffn_swiglu(x, sc, sh, wg, wu, *, tm=1024, tn=512):
    s, d = x.shape
    n = wg.shape[1]
    w_spec = pl.BlockSpec((d, tn), lambda i, j: (0, j))
    vec = pl.BlockSpec((1, d), lambda i, j: (0, 0))
    return pl.pallas_call(
        _ffn_swiglu_kernel,
        grid=(s // tm, n // tn),
        in_specs=[pl.BlockSpec((tm, d), lambda i, j: (i, 0)), vec, vec, w_spec, w_spec],
        out_specs=pl.BlockSpec((tm, tn), lambda i, j: (i, j)),
        out_shape=jax.ShapeDtypeStruct((s, n), BF16),
        scratch_shapes=[pltpu.VMEM((tm, d), BF16)],
        compiler_params=_params("parallel", "arbitrary"),
        name="ffn_swiglu",
    )(x, sc, sh, wg, wu)


def _in_proj_kernel(x_ref, sc_ref, sh_ref, wlat_ref, wval_ref, wgate_ref, wbrg_ref, lat_ref, u_ref, g_ref,
                    h_ref, *, n_lat, n_glu):
    j = pl.program_id(1)

    @pl.when(j == 0)
    def _():
        h = _layer_norm_rows(x_ref[...]) * (1.0 + sc_ref[...]) + sh_ref[...]
        h_ref[...] = h.astype(BF16)

    @pl.when(j < n_lat)
    def _():
        lat_ref[...] = jnp.dot(h_ref[...], wlat_ref[...], preferred_element_type=F32)

    @pl.when(jnp.logical_and(j >= n_lat, j < n_lat + n_glu))
    def _():
        h = h_ref[...]
        a = jnp.dot(h, wval_ref[...], preferred_element_type=F32)
        b = jnp.dot(h, wgate_ref[...], preferred_element_type=F32)
        u_ref[...] = (a * _sigmoid(b)).astype(u_ref.dtype)

    @pl.when(j >= n_lat + n_glu)
    def _():
        a = jnp.dot(h_ref[...], wbrg_ref[...], preferred_element_type=F32)
        g_ref[...] = _sigmoid(a).astype(g_ref.dtype)


def _in_proj(x, sc, sh, w_lat, w_val, w_gate, w_brg, *, tm=1024, tn=512, tn_lat=LATENT_COLS // 3):
    s, d = x.shape
    n_lat = w_lat.shape[1] // tn_lat
    n_glu = w_val.shape[1] // tn
    n_brg = w_brg.shape[1] // tn

    def lat_col(j):
        return jnp.minimum(j, n_lat - 1)

    def glu_col(j):
        return jnp.clip(j - n_lat, 0, n_glu - 1)

    def brg_col(j):
        return jnp.clip(j - n_lat - n_glu, 0, n_brg - 1)

    vec = pl.BlockSpec((1, d), lambda i, j: (0, 0))
    return pl.pallas_call(
        functools.partial(_in_proj_kernel, n_lat=n_lat, n_glu=n_glu),
        grid=(s // tm, n_lat + n_glu + n_brg),
        in_specs=[
            pl.BlockSpec((tm, d), lambda i, j: (i, 0)), vec, vec,
            pl.BlockSpec((d, tn_lat), lambda i, j: (0, lat_col(j))),
            pl.BlockSpec((d, tn), lambda i, j: (0, glu_col(j))),
            pl.BlockSpec((d, tn), lambda i, j: (0, glu_col(j))),
            pl.BlockSpec((d, tn), lambda i, j: (0, brg_col(j))),
        ],
        out_specs=[
            pl.BlockSpec((tm, tn_lat), lambda i, j: (i, lat_col(j))),
            pl.BlockSpec((tm, tn), lambda i, j: (i, glu_col(j))),
            pl.BlockSpec((tm, tn), lambda i, j: (i, brg_col(j))),
        ],
        out_shape=[
            jax.ShapeDtypeStruct((s, w_lat.shape[1]), F32),
            jax.ShapeDtypeStruct((s, w_val.shape[1]), BF16),
            jax.ShapeDtypeStruct((s, w_brg.shape[1]), BF16),
        ],
        scratch_shapes=[pltpu.VMEM((tm, d), BF16)],
        compiler_params=_params("parallel", "arbitrary"),
        name="in_proj",
    )(x, sc, sh, w_lat, w_val, w_gate, w_brg)


def _conv_pw_kernel(prev_ref, cur_ref, wdw_ref, bdw_ref, lng_ref, lnb_ref, wpw_ref, o_ref,
                    ext_ref, conv_ref, act_ref, *, tm, cw, rh):
    i = pl.program_id(0)

    @pl.when(pl.program_id(1) == 0)
    def _():
        d = cur_ref.shape[1]
        halo = prev_ref[...].astype(F32)
        ext_ref[0:CONV_HALO, :] = jnp.where(i > 0, halo, 0.0)
        ext_ref[CONV_HALO:CONV_HALO + tm, :] = cur_ref[...].astype(F32)
        first = CONV_HALO - (CONV_WIDTH - 1)

        def chan_chunk(ci, carry):
            cs = pl.ds(pl.multiple_of(ci * cw, cw), cw)
            for r0 in range(0, tm, rh):
                partial = []
                for res in range(SUBLANES):
                    taps = [k for k in range(CONV_WIDTH) if (first + k) % SUBLANES == res]
                    lo = min(first + k for k in taps) - res
                    hi = max(first + k for k in taps) - res
                    shifted = ext_ref[r0 + lo + res:r0 + hi + res + rh, cs]
                    acc = None
                    for k in taps:
                        off = first + k - res - lo
                        prod = wdw_ref[k:k + 1, cs] * shifted[off:off + rh]
                        acc = prod if acc is None else acc + prod
                    partial.append(acc)
                while len(partial) > 1:
                    partial = [a + b for a, b in zip(partial[0::2], partial[1::2])]
                conv_ref[r0:r0 + rh, cs] = partial[0]
            return carry

        lax.fori_loop(0, d // cw, chan_chunk, 0)
        u = conv_ref[...] + bdw_ref[...]
        u = _layer_norm_rows(u) * lng_ref[...] + lnb_ref[...]
        act_ref[...] = (u * _sigmoid(u)).astype(BF16)

    o_ref[...] = jnp.dot(act_ref[...], wpw_ref[...], preferred_element_type=F32).astype(o_ref.dtype)


def _conv_pw(u, w_dw, b_dw, ln_g, ln_b, w_pw, *, tm=256, tn=2048, cw=256, rh=64):
    s, d = u.shape
    n = w_pw.shape[1]
    ratio = tm // CONV_HALO
    vec = pl.BlockSpec((1, d), lambda i, j: (0, 0))
    return pl.pallas_call(
        functools.partial(_conv_pw_kernel, tm=tm, cw=cw, rh=rh),
        grid=(s // tm, n // tn),
        in_specs=[
            pl.BlockSpec((CONV_HALO, d), lambda i, j: (jnp.maximum(i * ratio - 1, 0), 0)),
            pl.BlockSpec((tm, d), lambda i, j: (i, 0)),
            pl.BlockSpec((CONV_WIDTH, d), lambda i, j: (0, 0)),
            vec, vec, vec,
            pl.BlockSpec((d, tn), lambda i, j: (0, j)),
        ],
        out_specs=pl.BlockSpec((tm, tn), lambda i, j: (i, j)),
        out_shape=jax.ShapeDtypeStruct((s, n), BF16),
        scratch_shapes=[
            pltpu.VMEM((CONV_HALO + tm, d), F32),
            pltpu.VMEM((tm, d), F32),
            pltpu.VMEM((tm, d), BF16),
        ],
        compiler_params=_params("parallel", "arbitrary"),
        name="conv_pw",
    )(u, u, w_dw, b_dw, ln_g, ln_b, w_pw)


def _rms_rows(x, g):
    return x * lax.rsqrt(jnp.mean(x * x, axis=-1, keepdims=True) + RMS_EPS) * g


_NT = (((1,), (1,)), ((), ()))


def _qproj_kernel(cq_ref, g_ref, wt_ref, wrot_t_ref, cos_t_ref, sin_t_ref, o_ref, *, scale):
    cn = _rms_rows(cq_ref[...], g_ref[...]).astype(BF16)
    main_t = lax.dot_general(wt_ref[...], cn, _NT, preferred_element_type=F32)
    rot_t = lax.dot_general(wrot_t_ref[...], cn, _NT, preferred_element_type=F32)
    cos_t = cos_t_ref[...]
    sin_t = sin_t_ref[...]
    n_nope = N_HEADS * QK_NOPE_DIM
    for h in range(N_HEADS):
        nope = main_t[h * QK_NOPE_DIM:(h + 1) * QK_NOPE_DIM]
        r0 = h * QK_ROPE_DIM
        rope = main_t[n_nope + r0:n_nope + r0 + QK_ROPE_DIM] * cos_t + rot_t[r0:r0 + QK_ROPE_DIM] * sin_t
        o_ref[h, 0:QK_NOPE_DIM, :] = (nope * scale).astype(BF16)
        o_ref[h, QK_NOPE_DIM:QK_HEAD_DIM, :] = (rope * scale).astype(BF16)


def _qproj(lat, g_q, wq_t, wq_rot_t, cos_t, sin_t, *, tm=256):
    s = lat.shape[0]
    scale = QK_HEAD_DIM ** -0.5 * 1.4426950408889634
    return pl.pallas_call(
        functools.partial(_qproj_kernel, scale=scale),
        grid=(s // tm,),
        in_specs=[
            pl.BlockSpec((tm, Q_LORA), lambda i: (i, 0)),
            pl.BlockSpec((1, Q_LORA), lambda i: (0, 0)),
            pl.BlockSpec(wq_t.shape, lambda i: (0, 0)),
            pl.BlockSpec(wq_rot_t.shape, lambda i: (0, 0)),
            pl.BlockSpec((QK_ROPE_DIM, tm), lambda i: (0, i)),
            pl.BlockSpec((QK_ROPE_DIM, tm), lambda i: (0, i)),
        ],
        out_specs=pl.BlockSpec((N_HEADS, QK_HEAD_DIM, tm), lambda i: (0, 0, i)),
        out_shape=jax.ShapeDtypeStruct((N_HEADS, QK_HEAD_DIM, s), BF16),
        compiler_params=_params("parallel"),
        name="q_proj",
    )(lat, g_q, wq_t, wq_rot_t, cos_t, sin_t)


def _kvproj_kernel(ckv_ref, kr_ref, g_ref, wk_ref, wv_t_ref, cos_ref, sin_ref, k_ref, vt_ref):
    cn = _rms_rows(ckv_ref[...], g_ref[...]).astype(BF16)
    kr = kr_ref[:, 0:QK_ROPE_DIM]
    kr_rot = kr_ref[:, QK_ROPE_DIM:2 * QK_ROPE_DIM]
    krope = (kr * cos_ref[...] + kr_rot * sin_ref[...]).astype(BF16)
    k_all = jnp.dot(cn, wk_ref[...], preferred_element_type=F32)
    v_t = lax.dot_general(wv_t_ref[...], cn, _NT, preferred_element_type=F32)
    for h in range(N_HEADS):
        k_ref[h, :, 0:QK_NOPE_DIM] = k_all[:, h * QK_NOPE_DIM:(h + 1) * QK_NOPE_DIM].astype(BF16)
        k_ref[h, :, QK_NOPE_DIM:QK_HEAD_DIM] = krope
        vt_ref[h] = v_t[h * V_HEAD_DIM:(h + 1) * V_HEAD_DIM].astype(BF16)


def _kvproj(lat, g_kv, wk, wv_t, cos, sin, *, tm=256):
    s = lat.shape[0]
    kr_block = (Q_LORA + KV_LORA) // (2 * QK_ROPE_DIM)
    return pl.pallas_call(
        _kvproj_kernel,
        grid=(s // tm,),
        in_specs=[
            pl.BlockSpec((tm, KV_LORA), lambda i: (i, Q_LORA // KV_LORA)),
            pl.BlockSpec((tm, 2 * QK_ROPE_DIM), lambda i: (i, kr_block)),
            pl.BlockSpec((1, KV_LORA), lambda i: (0, 0)),
            pl.BlockSpec(wk.shape, lambda i: (0, 0)),
            pl.BlockSpec(wv_t.shape, lambda i: (0, 0)),
            pl.BlockSpec((tm, QK_ROPE_DIM), lambda i: (i, 0)),
            pl.BlockSpec((tm, QK_ROPE_DIM), lambda i: (i, 0)),
        ],
        out_specs=[
            pl.BlockSpec((N_HEADS, tm, QK_HEAD_DIM), lambda i: (0, i, 0)),
            pl.BlockSpec((N_HEADS, V_HEAD_DIM, tm), lambda i: (0, 0, i)),
        ],
        out_shape=[
            jax.ShapeDtypeStruct((N_HEADS, s, QK_HEAD_DIM), BF16),
            jax.ShapeDtypeStruct((N_HEADS, V_HEAD_DIM, s), BF16),
        ],
        compiler_params=_params("parallel"),
        name="kv_proj",
    )(lat, lat, g_kv, wk, wv_t, cos, sin)


def _flash_kernel(qt_ref, k_ref, vt_ref, o_ref, sa_ref, sb_ref, m_ref, l_ref, acc_ref, *, tq, tk):
    qi = pl.program_id(1)
    qt = qt_ref[0]

    def scores(j, s_ref):
        start = pl.multiple_of(j * tk, tk)
        s_ref[...] = jnp.dot(k_ref[0, pl.ds(start, tk), :], qt, preferred_element_type=F32)

    def consume(j, s_ref, mask):
        s = s_ref[...]
        if mask is not None:
            s = jnp.where(mask, s, -jnp.inf)
        m_prev = m_ref[...]
        m_new = jnp.maximum(m_prev, jnp.max(s, axis=0, keepdims=True))
        alpha = jnp.exp2(m_prev - m_new)
        p = jnp.exp2(s - m_new)
        l_ref[...] = alpha * l_ref[...] + jnp.sum(p, axis=0, keepdims=True)
        start = pl.multiple_of(j * tk, tk)
        pv = jnp.dot(vt_ref[0, :, pl.ds(start, tk)], p.astype(BF16), preferred_element_type=F32)
        acc_ref[...] = alpha * acc_ref[...] + pv
        m_ref[...] = m_new

    m_ref[...] = jnp.full(m_ref.shape, -jnp.inf, F32)
    l_ref[...] = jnp.zeros(l_ref.shape, F32)
    acc_ref[...] = jnp.zeros(acc_ref.shape, F32)

    scores(0, sa_ref)

    def pair(tt, carry):
        j = 2 * tt
        scores(j + 1, sb_ref)
        consume(j, sa_ref, None)
        scores(j + 2, sa_ref)
        consume(j + 1, sb_ref, None)
        return carry

    lax.fori_loop(0, qi, pair, 0)

    key_chunk = lax.broadcasted_iota(jnp.int32, (tk, tq), 0) // CHUNK
    query_chunk = lax.broadcasted_iota(jnp.int32, (tk, tq), 1) // CHUNK
    j_diag = 2 * qi
    scores(j_diag + 1, sb_ref)
    consume(j_diag, sa_ref, key_chunk <= query_chunk)
    consume(j_diag + 1, sb_ref, key_chunk + tk // CHUNK <= query_chunk)

    out_t = acc_ref[...] / l_ref[...]
    o_ref[...] = out_t.T.astype(o_ref.dtype)


def _flash(q_t, k, v_t, *, tq=1024, tk=512):
    assert tq == 2 * tk
    h, _, s = q_t.shape
    return pl.pallas_call(
        functools.partial(_flash_kernel, tq=tq, tk=tk),
        grid=(h, s // tq),
        in_specs=[
            pl.BlockSpec((1, QK_HEAD_DIM, tq), lambda hh, i: (hh, 0, i)),
            pl.BlockSpec((1, s, QK_HEAD_DIM), lambda hh, i: (hh, 0, 0)),
            pl.BlockSpec((1, V_HEAD_DIM, s), lambda hh, i: (hh, 0, 0)),
        ],
        out_specs=pl.BlockSpec((tq, V_HEAD_DIM), lambda hh, i: (i, hh)),
        out_shape=jax.ShapeDtypeStruct((s, h * V_HEAD_DIM), BF16),
        scratch_shapes=[
            pltpu.VMEM((tk, tq), F32),
            pltpu.VMEM((tk, tq), F32),
            pltpu.VMEM((1, tq), F32),
            pltpu.VMEM((1, tq), F32),
            pltpu.VMEM((V_HEAD_DIM, tq), F32),
        ],
        compiler_params=_params("parallel", "arbitrary"),
        name="flash_attn",
    )(q_t, k, v_t)


def _wo_mix_kernel(o_ref, w_ref, gc_ref, gm_ref, yc_ref, out_ref):
    y_mla = jnp.dot(o_ref[...], w_ref[...], preferred_element_type=F32)
    mixed = gc_ref[...].astype(F32) * yc_ref[...].astype(F32) + gm_ref[...].astype(F32) * y_mla
    out_ref[...] = mixed.astype(out_ref.dtype)


def _wo_mix(o, w_o, gates, y_conv, *, tm=1024, tn=512):
    s, d = o.shape
    n = w_o.shape[1]
    nb = n // tn
    return pl.pallas_call(
        _wo_mix_kernel,
        grid=(s // tm, nb),
        in_specs=[
            pl.BlockSpec((tm, d), lambda i, j: (i, 0)),
            pl.BlockSpec((d, tn), lambda i, j: (0, j)),
            pl.BlockSpec((tm, tn), lambda i, j: (i, j)),
            pl.BlockSpec((tm, tn), lambda i, j: (i, j + nb)),
            pl.BlockSpec((tm, tn), lambda i, j: (i, j)),
        ],
        out_specs=pl.BlockSpec((tm, tn), lambda i, j: (i, j)),
        out_shape=jax.ShapeDtypeStruct((s, n), BF16),
        compiler_params=_params("parallel", "parallel"),
        name="wo_mix",
    )(o, w_o, gates, gates, y_conv)


def _resid_ln(x, f, gate, lng, lnb):
    y = DEEPNORM_ALPHA * x + (1.0 + gate) * f
    return _layer_norm_rows(y) * lng + lnb


def _mm_resid_ln_kernel(a_ref, w_ref, x_ref, gate_ref, lng_ref, lnb_ref, o_ref, acc_ref):
    kk = pl.program_id(1)

    @pl.when(kk == 0)
    def _():
        acc_ref[...] = jnp.zeros(acc_ref.shape, F32)

    acc_ref[...] += jnp.dot(a_ref[...], w_ref[...], preferred_element_type=F32)

    @pl.when(kk == pl.num_programs(1) - 1)
    def _():
        o_ref[...] = _resid_ln(x_ref[...], acc_ref[...], gate_ref[...], lng_ref[...], lnb_ref[...])


def _mm_resid_ln(a, w, x, gate, lng, lnb, *, tm=512, tk, name):
    s, kdim = a.shape
    d = w.shape[1]
    vec = pl.BlockSpec((1, d), lambda i, k: (0, 0))
    return pl.pallas_call(
        _mm_resid_ln_kernel,
        grid=(s // tm, kdim // tk),
        in_specs=[
            pl.BlockSpec((tm, tk), lambda i, k: (i, k)),
            pl.BlockSpec((tk, d), lambda i, k: (k, 0)),
            pl.BlockSpec((tm, d), lambda i, k: (i, 0)),
            vec, vec, vec,
        ],
        out_specs=pl.BlockSpec((tm, d), lambda i, k: (i, 0)),
        out_shape=jax.ShapeDtypeStruct((s, d), F32),
        scratch_shapes=[pltpu.VMEM((tm, d), F32)],
        compiler_params=_params("parallel", "arbitrary"),
        name=name,
    )(a, w, x, gate, lng, lnb)


def _router_kernel(x_ref, sc_ref, sh_ref, wr_ref, h_ref, comb_ref, sel_ref):
    h = _layer_norm_rows(x_ref[...]) * (1.0 + sc_ref[...]) + sh_ref[...]
    h_ref[...] = _pack_bf16_pair(h)
    logits = jnp.dot(h, wr_ref[...], preferred_element_type=F32, precision=lax.Precision.HIGHEST)
    lane = lax.broadcasted_iota(jnp.int32, logits.shape, 1)
    neg = -jnp.inf
    l1 = jnp.where(lane < N_EXPERTS, logits, neg)
    v1 = jnp.max(l1, axis=1, keepdims=True)
    i1 = jnp.min(jnp.where(l1 == v1, lane, LANES), axis=1, keepdims=True)
    l2 = jnp.where(lane == i1, neg, l1)
    v2 = jnp.max(l2, axis=1, keepdims=True)
    i2 = jnp.min(jnp.where(l2 == v2, lane, LANES), axis=1, keepdims=True)
    e2 = jnp.exp(v2 - v1)
    w1 = 1.0 / (1.0 + e2)
    w2 = e2 / (1.0 + e2)
    comb_ref[...] = jnp.where(lane == i1, w1, 0.0) + jnp.where(lane == i2, w2, 0.0)
    sel_ref[...] = jnp.where(lane == i1, 1.0, 0.0) + jnp.where(lane == i2, 2.0, 0.0)


def _router(x, sc, sh, wr_pad, *, tm=256):
    s, d = x.shape
    vec = pl.BlockSpec((1, d), lambda i: (0, 0))
    lane_blk = pl.BlockSpec((tm, LANES), lambda i: (i, 0))
    return pl.pallas_call(
        _router_kernel,
        grid=(s // tm,),
        in_specs=[pl.BlockSpec((tm, d), lambda i: (i, 0)), vec, vec,
                  pl.BlockSpec((d, LANES), lambda i: (0, 0))],
        out_specs=[pl.BlockSpec((tm, d // 2), lambda i: (i, 0)), lane_blk, lane_blk],
        out_shape=[jax.ShapeDtypeStruct((s, d // 2), jnp.uint32), jax.ShapeDtypeStruct((s, LANES), F32),
                   jax.ShapeDtypeStruct((s, LANES), F32)],
        compiler_params=_params("parallel"),
        name="router",
    )(x, sc, sh, wr_pad)


def _routing_plan(comb, sel, tm, n_tiles):
    sel8 = sel[:, :N_EXPERTS]
    comb8 = comb[:, :N_EXPERTS]
    chosen = (sel8 > 0).astype(jnp.int32)
    rank = jnp.cumsum(chosen, axis=0) - chosen
    counts = jnp.sum(chosen, axis=0)
    tiles_e = (counts + tm - 1) // tm
    tile_end = jnp.cumsum(tiles_e)
    row_start = (tile_end - tiles_e) * tm
    dest = row_start[None, :] + rank
    pos0 = jnp.sum(jnp.where(sel8 == 1.0, dest, 0), axis=1).astype(jnp.int32)
    pos1 = jnp.sum(jnp.where(sel8 == 2.0, dest, 0), axis=1).astype(jnp.int32)
    w0 = jnp.sum(jnp.where(sel8 == 1.0, comb8, 0.0), axis=1, keepdims=True)
    w1 = jnp.sum(jnp.where(sel8 == 2.0, comb8, 0.0), axis=1, keepdims=True)
    n_valid = tile_end[-1]
    tile_id = jnp.minimum(jnp.arange(n_tiles, dtype=jnp.int32), n_valid - 1)
    tile_expert = jnp.sum((tile_id[:, None] >= tile_end[None, :]).astype(jnp.int32), axis=1)
    meta = jnp.concatenate([tile_expert, n_valid[None]]).astype(jnp.int32)
    tok = jnp.arange(sel.shape[0], dtype=jnp.int32)
    src_token = jnp.zeros((n_tiles * tm,), jnp.int32).at[jnp.concatenate([pos0, pos1])].set(
        jnp.concatenate([tok, tok]), unique_indices=True)
    return pos0, pos1, w0, w1, meta, src_token


_HIGH_HALF = 0xFFFF0000


def _pack_bf16_pair(x):
    n = x.shape[1] // 2
    bits = lax.bitcast_convert_type(x.astype(BF16).astype(F32), jnp.uint32)
    return (bits[:, :n] >> 16) | (bits[:, n:] & jnp.uint32(_HIGH_HALF))


def _unpack_bf16_pair(w):
    lo = lax.bitcast_convert_type(w << 16, F32)
    hi = lax.bitcast_convert_type(w & jnp.uint32(_HIGH_HALF), F32)
    return lo, hi


def _row_copy(src_hbm, dst_hbm, src_row, dst_row, sem):
    return pltpu.make_async_copy(src_hbm.at[pl.ds(src_row, 1)], dst_hbm.at[pl.ds(dst_row, 1)], sem)


def _dispatch_kernel(meta_ref, src_ref, src_next_ref, h_hbm, o_ref, buf, sem, *, tb, n_tiles):
    i = pl.program_id(0)
    n_used = meta_ref[n_tiles]
    slot = i % 2

    def start_gather(idx_ref, dst_slot):
        def issue(r, carry):
            _row_copy(h_hbm, buf.at[dst_slot], idx_ref[0, 0, r], r, sem.at[dst_slot]).start()
            return carry

        lax.fori_loop(0, tb, issue, 0)

    @pl.when(jnp.logical_and(i == 0, n_used > 0))
    def _():
        start_gather(src_ref, 0)

    @pl.when(i + 1 < n_used)
    def _():
        start_gather(src_next_ref, 1 - slot)

    @pl.when(i < n_used)
    def _():
        def drain(r, carry):
            _row_copy(h_hbm, buf.at[slot], 0, 0, sem.at[slot]).wait()
            return carry

        lax.fori_loop(0, tb, drain, 0)
        lo, hi = _unpack_bf16_pair(buf[slot])
        half = lo.shape[1]
        o_ref[:, :half] = lo.astype(o_ref.dtype)
        o_ref[:, half:] = hi.astype(o_ref.dtype)

    @pl.when(i >= n_used)
    def _():
        o_ref[...] = jnp.zeros(o_ref.shape, o_ref.dtype)


def _dispatch(meta, src_token, h, *, tb):
    n_rows = src_token.shape[0]
    dp = h.shape[1]
    d = 2 * dp
    n_tiles = n_rows // tb
    return pl.pallas_call(
        functools.partial(_dispatch_kernel, tb=tb, n_tiles=n_tiles),
        grid_spec=pltpu.PrefetchScalarGridSpec(
            num_scalar_prefetch=1,
            grid=(n_tiles,),
            in_specs=[
                pl.BlockSpec((1, 1, tb), lambda i, m: (i, 0, 0), memory_space=pltpu.SMEM),
                pl.BlockSpec((1, 1, tb), lambda i, m: (jnp.minimum(i + 1, n_tiles - 1), 0, 0),
                             memory_space=pltpu.SMEM),
                pl.BlockSpec(memory_space=pl.ANY),
            ],
            out_specs=pl.BlockSpec((tb, d), lambda i, m: (i, 0)),
            scratch_shapes=[pltpu.VMEM((2, tb, dp), jnp.uint32), pltpu.SemaphoreType.DMA((2,))],
        ),
        out_shape=jax.ShapeDtypeStruct((n_rows, d), BF16),
        compiler_params=_params("arbitrary"),
        name="moe_dispatch",
    )(meta, src_token.reshape(n_tiles, 1, tb), src_token.reshape(n_tiles, 1, tb), h)


def _expert_swiglu_kernel(meta_ref, x_ref, wg_ref, wu_ref, o_ref, wg_bf, wu_bf, *, n_tiles):
    i = pl.program_id(1)

    @pl.when(jnp.logical_or(i == 0, meta_ref[i] != meta_ref[jnp.maximum(i - 1, 0)]))
    def _():
        wg_bf[...] = wg_ref[0, 0].astype(BF16)
        wu_bf[...] = wu_ref[0, 0].astype(BF16)

    @pl.when(i < meta_ref[n_tiles])
    def _():
        x = x_ref[...]
        a = jnp.dot(x, wg_bf[...], preferred_element_type=F32)
        b = jnp.dot(x, wu_bf[...], preferred_element_type=F32)
        o_ref[...] = (a * _sigmoid(a) * b).astype(o_ref.dtype)

    @pl.when(i >= meta_ref[n_tiles])
    def _():
        o_ref[...] = jnp.zeros(o_ref.shape, o_ref.dtype)


def _expert_swiglu(meta, xs, wg, wu, *, layer, tm, tn=1024):
    r, d = xs.shape
    f = wg.shape[3]
    n_tiles = r // tm
    w_spec = pl.BlockSpec((1, 1, d, tn), lambda j, i, m: (layer, m[i], 0, j))

    def row_tile(i, m):
        return jnp.minimum(i, m[n_tiles] - 1)

    return pl.pallas_call(
        functools.partial(_expert_swiglu_kernel, n_tiles=n_tiles),
        grid_spec=pltpu.PrefetchScalarGridSpec(
            num_scalar_prefetch=1,
            grid=(f // tn, n_tiles),
            in_specs=[
                pl.BlockSpec((tm, d), lambda j, i, m: (row_tile(i, m), 0)),
                w_spec,
                w_spec,
            ],
            out_specs=pl.BlockSpec((tm, tn), lambda j, i, m: (i, j)),
            scratch_shapes=[pltpu.VMEM((d, tn), BF16), pltpu.VMEM((d, tn), BF16)],
        ),
        out_shape=jax.ShapeDtypeStruct((r, f), BF16),
        compiler_params=pltpu.CompilerParams(dimension_semantics=("arbitrary", "arbitrary"),
                                             vmem_limit_bytes=EXPERT_VMEM_LIMIT),
        name="expert_swiglu",
    )(meta, xs, wg, wu)


def _expert_down_kernel(meta_ref, a_ref, w_ref, o_ref, acc_ref, *, n_tiles):
    i = pl.program_id(0)
    kk = pl.program_id(1)

    @pl.when(i < meta_ref[n_tiles])
    def _():
        @pl.when(kk == 0)
        def _():
            acc_ref[...] = jnp.zeros(acc_ref.shape, F32)

        acc_ref[...] += jnp.dot(a_ref[...], w_ref[0], preferred_element_type=F32)

        @pl.when(kk == pl.num_programs(1) - 1)
        def _():
            o_ref[...] = _pack_bf16_pair(acc_ref[...])

    @pl.when(jnp.logical_and(i >= meta_ref[n_tiles], kk == 0))
    def _():
        o_ref[...] = jnp.zeros(o_ref.shape, o_ref.dtype)


def _expert_down(meta, act, wd, *, tm, tk=1024):
    r, f = act.shape
    d = wd.shape[2]
    n_tiles = r // tm
    nk = f // tk

    def used(i, m):
        return i < m[n_tiles]

    def row_tile(i, m):
        return jnp.minimum(i, m[n_tiles] - 1)

    def k_tile(i, k, m):
        return jnp.where(used(i, m), k, nk - 1)

    return pl.pallas_call(
        functools.partial(_expert_down_kernel, n_tiles=n_tiles),
        grid_spec=pltpu.PrefetchScalarGridSpec(
            num_scalar_prefetch=1,
            grid=(n_tiles, nk),
            in_specs=[
                pl.BlockSpec((tm, tk), lambda i, k, m: (row_tile(i, m), k_tile(i, k, m))),
                pl.BlockSpec((1, tk, d), lambda i, k, m: (m[i], k_tile(i, k, m), 0)),
            ],
            out_specs=pl.BlockSpec((tm, d // 2), lambda i, k, m: (i, 0)),
            scratch_shapes=[pltpu.VMEM((tm, d), F32)],
        ),
        out_shape=jax.ShapeDtypeStruct((r, d // 2), jnp.uint32),
        compiler_params=_params("arbitrary", "arbitrary"),
        name="expert_down",
    )(meta, act, wd)


def _combine_ln_kernel(pos0_ref, pos1_ref, pos0_next_ref, pos1_next_ref, ys_hbm, w0_ref, w1_ref, x_ref,
                       gate_ref, lng_ref, lnb_ref, o_ref, a_buf, b_buf, sem, *, tb):
    i = pl.program_id(0)
    slot = i % 2

    def start_gather(p0_ref, p1_ref, dst_slot):
        def issue(r, carry):
            _row_copy(ys_hbm, a_buf.at[dst_slot], p0_ref[0, 0, r], r, sem.at[dst_slot]).start()
            _row_copy(ys_hbm, b_buf.at[dst_slot], p1_ref[0, 0, r], r, sem.at[dst_slot]).start()
            return carry

        lax.fori_loop(0, tb, issue, 0)

    @pl.when(i == 0)
    def _():
        start_gather(pos0_ref, pos1_ref, 0)

    @pl.when(i + 1 < pl.num_programs(0))
    def _():
        start_gather(pos0_next_ref, pos1_next_ref, 1 - slot)

    def drain(r, carry):
        _row_copy(ys_hbm, a_buf.at[slot], 0, 0, sem.at[slot]).wait()
        _row_copy(ys_hbm, b_buf.at[slot], 0, 0, sem.at[slot]).wait()
        return carry

    lax.fori_loop(0, tb, drain, 0)
    a_lo, a_hi = _unpack_bf16_pair(a_buf[slot])
    b_lo, b_hi = _unpack_bf16_pair(b_buf[slot])
    w0 = w0_ref[...]
    w1 = w1_ref[...]
    f = jnp.concatenate([w0 * a_lo + w1 * b_lo, w0 * a_hi + w1 * b_hi], axis=1)
    o_ref[...] = _resid_ln(x_ref[...], f, gate_ref[...], lng_ref[...], lnb_ref[...])


def _combine_ln(ys, pos0, pos1, w0, w1, x, gate, lng, lnb, *, tb=256):
    s, d = x.shape
    n_blk = s // tb
    idx_blk = pl.BlockSpec((1, 1, tb), lambda i: (i, 0, 0), memory_space=pltpu.SMEM)
    idx_next = pl.BlockSpec((1, 1, tb), lambda i: (jnp.minimum(i + 1, n_blk - 1), 0, 0), memory_space=pltpu.SMEM)
    vec = pl.BlockSpec((1, d), lambda i: (0, 0))
    col = pl.BlockSpec((tb, 1), lambda i: (i, 0))
    blk = pl.BlockSpec((tb, d), lambda i: (i, 0))
    p0 = pos0.reshape(n_blk, 1, tb)
    p1 = pos1.reshape(n_blk, 1, tb)
    return pl.pallas_call(
        functools.partial(_combine_ln_kernel, tb=tb),
        grid=(n_blk,),
        in_specs=[idx_blk, idx_blk, idx_next, idx_next, pl.BlockSpec(memory_space=pl.ANY),
                  col, col, blk, vec, vec, vec],
        out_specs=blk,
        out_shape=jax.ShapeDtypeStruct((s, d), F32),
        scratch_shapes=[pltpu.VMEM((2, tb, d // 2), jnp.uint32), pltpu.VMEM((2, tb, d // 2), jnp.uint32),
                        pltpu.SemaphoreType.DMA((2,))],
        compiler_params=_params("arbitrary"),
        name="moe_combine_ln",
    )(p0, p1, p0, p1, ys, w0, w1, x, gate, lng, lnb)


def _rotate_half_cols(w):
    half = w.shape[-1] // 2
    return jnp.concatenate([-w[..., half:], w[..., :half]], axis=-1)


def kernel(x, c, positions, w_ada, b_ada, w_in, g_q, w_uq, g_kv, w_ukv, w_o, w_dw, b_dw, conv_ln_g, conv_ln_b, w_pw, w_out, ln1_g, ln1_b, ln2_g, ln2_b, w_ff_gate, w_ff_up, w_ff_down, w_router, w_e_gate, w_e_up, w_e_down):
    assert x.shape == (1, SEQ, D_MODEL) and c.shape == (1, D_MODEL)
    d = D_MODEL
    xs = x.reshape(SEQ, d)

    mod = _ada(c.reshape(d, 1), w_ada, b_ada).reshape(DEPTH, 6, 1, d)

    inv_freq = ROPE_THETA ** (-jnp.arange(0, QK_ROPE_DIM, 2, dtype=F32) / QK_ROPE_DIM)
    inv_freq2 = jnp.concatenate([inv_freq, inv_freq])
    cos, sin, cos_t, sin_t = _rope_tables(positions.reshape(SEQ), inv_freq2)

    row = lambda v: v.reshape(1, -1)

    for l in range(DEPTH):
        sh1, sc1, g1, sh2, sc2, g2 = (mod[l, t] for t in range(6))
        wl = w_in[l]
        o_kr = Q_LORA + KV_LORA
        o_val = o_kr + QK_ROPE_DIM
        w_kr = wl[:, o_kr:o_val]
        w_lat = jnp.concatenate([wl[:, :o_val], _rotate_half_cols(w_kr)], axis=1).astype(BF16)
        w_val = wl[:, o_val:o_val + d].astype(BF16)
        w_gate = wl[:, o_val + d:o_val + 2 * d].astype(BF16)
        w_brg = wl[:, o_val + 2 * d:].astype(BF16)

        lat, u, gates = _in_proj(xs, sc1, sh1, w_lat, w_val, w_gate, w_brg)

        y_conv = _conv_pw(u, w_dw[l], row(b_dw[l]), row(conv_ln_g[l]), row(conv_ln_b[l]), w_pw[l].astype(BF16))

        wq = w_uq[l].reshape(Q_LORA, N_HEADS, QK_HEAD_DIM)
        wq_rope = wq[..., QK_NOPE_DIM:]
        wq_t = jnp.concatenate([wq[..., :QK_NOPE_DIM].reshape(Q_LORA, -1), wq_rope.reshape(Q_LORA, -1)], axis=1).T
        wq_rot_t = _rotate_half_cols(wq_rope).reshape(Q_LORA, -1).T
        wkv = w_ukv[l].reshape(KV_LORA, N_HEADS, QK_NOPE_DIM + V_HEAD_DIM)
        wk = wkv[..., :QK_NOPE_DIM].reshape(KV_LORA, -1)
        wv_t = wkv[..., QK_NOPE_DIM:].reshape(KV_LORA, -1).T
        q_t = _qproj(lat, row(g_q[l]), wq_t.astype(BF16), wq_rot_t.astype(BF16), cos_t, sin_t)
        k, v_t = _kvproj(lat, row(g_kv[l]), wk.astype(BF16), wv_t.astype(BF16), cos, sin)
        o = _flash(q_t, k, v_t)
        mixed = _wo_mix(o, w_o[l].astype(BF16), gates, y_conv)
        xs = _mm_resid_ln(mixed, w_out[l].astype(BF16), xs, g1, row(ln1_g[l]), row(ln1_b[l]), tk=d, name="out_proj_ln")

        i = l // 2
        if l % 2 == 0:
            act = _ffn_swiglu(xs, sc2, sh2, w_ff_gate[i].astype(BF16), w_ff_up[i].astype(BF16))
            xs = _mm_resid_ln(act, w_ff_down[i].astype(BF16), xs, g2, row(ln2_g[l]), row(ln2_b[l]), tk=D_FF_DENSE // 4, name="ffn_down_ln")
        else:
            wr_pad = jnp.zeros((d, LANES), F32).at[:, :N_EXPERTS].set(w_router[i])
            h2, comb, sel = _router(xs, sc2, sh2, wr_pad)
            tm_e = EXPERT_ROW_TILE
            n_tiles = (2 * SEQ) // tm_e + N_EXPERTS
            pos0, pos1, w0, w1, meta, src_token = _routing_plan(comb, sel, tm_e, n_tiles)
            x_sorted = _dispatch(meta, src_token, h2, tb=tm_e)
            act = _expert_swiglu(meta, x_sorted, w_e_gate, w_e_up, layer=i, tm=tm_e)
            y_sorted = _expert_down(meta, act, w_e_down[i].astype(BF16), tm=tm_e)
            xs = _combine_ln(y_sorted, pos0, pos1, w0, w1, xs, g2, row(ln2_g[l]), row(ln2_b[l]))

    return xs.reshape(1, SEQ, D_MODEL)
```

```python
import functools

import jax
import jax.numpy as jnp
from jax import lax
from jax.experimental import pallas as pl
from jax.experimental.pallas import tpu as pltpu

F32 = jnp.float32
BF16 = jnp.bfloat16

D_MODEL = 2048
SEQ = 8192
DEPTH = 2
CHUNK = 64
N_HEADS = 16
QK_NOPE_DIM = 128
QK_ROPE_DIM = 64
V_HEAD_DIM = 128
QK_HEAD_DIM = QK_NOPE_DIM + QK_ROPE_DIM
Q_LORA = 512
KV_LORA = 512
ROPE_THETA = 10000.0
CONV_WIDTH = 31
D_FF_DENSE = 5632
N_EXPERTS = 8
D_FF_EXPERT = 7168
DEEPNORM_ALPHA = (2.0 * DEPTH) ** 0.25
LN_EPS = 1e-5
RMS_EPS = 1e-6
LATENT_COLS = Q_LORA + KV_LORA + 2 * QK_ROPE_DIM
CONV_HALO = 32
LANES = 128
SUBLANES = 8
EXPERT_ROW_TILE = 512

VMEM_LIMIT = 48 * 1024 * 1024
EXPERT_VMEM_LIMIT = 58 * 1024 * 1024


def _params(*sem):
    return pltpu.CompilerParams(dimension_semantics=sem, vmem_limit_bytes=VMEM_LIMIT)


def _layer_norm_rows(x):
    mu = jnp.mean(x, axis=-1, keepdims=True)
    xc = x - mu
    var = jnp.mean(xc * xc, axis=-1, keepdims=True)
    return xc * lax.rsqrt(var + LN_EPS)


def _sigmoid(x):
    return 1.0 / (1.0 + jnp.exp(-x))


ADA_STREAMS = 4


def _ada_kernel(c_ref, *rest, tk):
    w_refs, (b_ref, o_ref) = rest[:ADA_STREAMS], rest[ADA_STREAMS:]
    k = pl.program_id(1)

    @pl.when(k == 0)
    def _():
        o_ref[0] = b_ref[0]

    part = None
    for q, w_ref in enumerate(w_refs):
        c = c_ref[pl.ds(pl.multiple_of((k * ADA_STREAMS + q) * tk, tk), tk), :]
        term = jnp.sum(c * _sigmoid(c) * w_ref[0], axis=0, keepdims=True)
        part = term if part is None else part + term
    o_ref[0] += part


def _ada(c_col, w_ada, b_ada, tk=64):
    depth, d, n = w_ada.shape
    w_specs = [pl.BlockSpec((1, tk, n), functools.partial(lambda l, k, q: (l, k * ADA_STREAMS + q, 0), q=q))
               for q in range(ADA_STREAMS)]
    return pl.pallas_call(
        functools.partial(_ada_kernel, tk=tk),
        grid=(depth, d // (tk * ADA_STREAMS)),
        in_specs=[pl.BlockSpec((d, 1), lambda l, k: (0, 0))] + w_specs
        + [pl.BlockSpec((1, 1, n), lambda l, k: (l, 0, 0))],
        out_specs=pl.BlockSpec((1, 1, n), lambda l, k: (l, 0, 0)),
        out_shape=jax.ShapeDtypeStruct((depth, 1, n), F32),
        compiler_params=_params("parallel", "arbitrary"),
        name="ada_mod",
    )(c_col, *([w_ada] * ADA_STREAMS), b_ada.reshape(depth, 1, n))


def _rope_kernel(pos_col_ref, pos_row_ref, inv_row_ref, inv_col_ref, cos_ref, sin_ref, cos_t_ref, sin_t_ref):
    ang = pos_col_ref[...].astype(F32) * inv_row_ref[...]
    cos_ref[...] = jnp.cos(ang)
    sin_ref[...] = jnp.sin(ang)
    ang_t = inv_col_ref[...] * pos_row_ref[...].astype(F32)
    cos_t_ref[...] = jnp.cos(ang_t)
    sin_t_ref[...] = jnp.sin(ang_t)


def _rope_tables(positions, inv_freq2, ts=512):
    s = positions.shape[0]
    r = inv_freq2.shape[0]
    return pl.pallas_call(
        _rope_kernel,
        grid=(s // ts,),
        in_specs=[
            pl.BlockSpec((ts, 1), lambda i: (i, 0)),
            pl.BlockSpec((1, ts), lambda i: (0, i)),
            pl.BlockSpec((1, r), lambda i: (0, 0)),
            pl.BlockSpec((r, 1), lambda i: (0, 0)),
        ],
        out_specs=[pl.BlockSpec((ts, r), lambda i: (i, 0))] * 2 + [pl.BlockSpec((r, ts), lambda i: (0, i))] * 2,
        out_shape=[jax.ShapeDtypeStruct((s, r), F32)] * 2 + [jax.ShapeDtypeStruct((r, s), F32)] * 2,
        compiler_params=_params("parallel"),
        name="rope_tables",
    )(positions.reshape(s, 1), positions.reshape(1, s), inv_freq2.reshape(1, r), inv_freq2.reshape(r, 1))


def _ffn_swiglu_kernel(x_ref, sc_ref, sh_ref, wg_ref, wu_ref, o_ref, h_ref):
    @pl.when(pl.program_id(1) == 0)
    def _():
        h = _layer_norm_rows(x_ref[...]) * (1.0 + sc_ref[...]) + sh_ref[...]
        h_ref[...] = h.astype(BF16)

    h = h_ref[...]
    a = jnp.dot(h, wg_ref[...], preferred_element_type=F32)
    b = jnp.dot(h, wu_ref[...], preferred_element_type=F32)
    o_ref[...] = (a * _sigmoid(a) * b).astype(o_ref.dtype)


def _ffn_swiglu(x, sc, sh, wg, wu, *, tm=1024, tn=512):
    s, d = x.shape
    n = wg.shape[1]
    w_spec = pl.BlockSpec((d, tn), lambda i, j: (0, j))
    vec = pl.BlockSpec((1, d), lambda i, j: (0, 0))
    return pl.pallas_call(
        _ffn_swiglu_kernel,
        grid=(s // tm, n // tn),
        in_specs=[pl.BlockSpec((tm, d), lambda i, j: (i, 0)), vec, vec, w_spec, w_spec],
        out_specs=pl.BlockSpec((tm, tn), lambda i, j: (i, j)),
        out_shape=jax.ShapeDtypeStruct((s, n), BF16),
        scratch_shapes=[pltpu.VMEM((tm, d), BF16)],
        compiler_params=_params("parallel", "arbitrary"),
        name="ffn_swiglu",
    )(x, sc, sh, wg, wu)


def _in_proj_kernel(x_ref, sc_ref, sh_ref, wlat_ref, wval_ref, wgate_ref, wbrg_ref, lat_ref, u_ref, g_ref,
                    h_ref, *, n_lat, n_glu):
    j = pl.program_id(1)

    @pl.when(j == 0)
    def _():
        h = _layer_norm_rows(x_ref[...]) * (1.0 + sc_ref[...]) + sh_ref[...]
        h_ref[...] = h.astype(BF16)

    @pl.when(j < n_lat)
    def _():
        lat_ref[...] = jnp.dot(h_ref[...], wlat_ref[...], preferred_element_type=F32)

    @pl.when(jnp.logical_and(j >= n_lat, j < n_lat + n_glu))
    def _():
        h = h_ref[...]
        a = jnp.dot(h, wval_ref[...], preferred_element_type=F32)
        b = jnp.dot(h, wgate_ref[...], preferred_element_type=F32)
        u_ref[...] = (a * _sigmoid(b)).astype(u_ref.dtype)

    @pl.when(j >= n_lat + n_glu)
    def _():
        a = jnp.dot(h_ref[...], wbrg_ref[...], preferred_element_type=F32)
        g_ref[...] = _sigmoid(a).astype(g_ref.dtype)


def _in_proj(x, sc, sh, w_lat, w_val, w_gate, w_brg, *, tm=1024, tn=512, tn_lat=LATENT_COLS // 3):
    s, d = x.shape
    n_lat = w_lat.shape[1] // tn_lat
    n_glu = w_val.shape[1] // tn
    n_brg = w_brg.shape[1] // tn

    def lat_col(j):
        return jnp.minimum(j, n_lat - 1)

    def glu_col(j):
        return jnp.clip(j - n_lat, 0, n_glu - 1)

    def brg_col(j):
        return jnp.clip(j - n_lat - n_glu, 0, n_brg - 1)

    vec = pl.BlockSpec((1, d), lambda i, j: (0, 0))
    return pl.pallas_call(
        functools.partial(_in_proj_kernel, n_lat=n_lat, n_glu=n_glu),
        grid=(s // tm, n_lat + n_glu + n_brg),
        in_specs=[
            pl.BlockSpec((tm, d), lambda i, j: (i, 0)), vec, vec,
            pl.BlockSpec((d, tn_lat), lambda i, j: (0, lat_col(j))),
            pl.BlockSpec((d, tn), lambda i, j: (0, glu_col(j))),
            pl.BlockSpec((d, tn), lambda i, j: (0, glu_col(j))),
            pl.BlockSpec((d, tn), lambda i, j: (0, brg_col(j))),
        ],
        out_specs=[
            pl.BlockSpec((tm, tn_lat), lambda i, j: (i, lat_col(j))),
            pl.BlockSpec((tm, tn), lambda i, j: (i, glu_col(j))),
            pl.BlockSpec((tm, tn), lambda i, j: (i, brg_col(j))),
        ],
        out_shape=[
            jax.ShapeDtypeStruct((s, w_lat.shape[1]), F32),
            jax.ShapeDtypeStruct((s, w_val.shape[1]), BF16),
            jax.ShapeDtypeStruct((s, w_brg.shape[1]), BF16),
        ],
        scratch_shapes=[pltpu.VMEM((tm, d), BF16)],
        compiler_params=_params("parallel", "arbitrary"),
        name="in_proj",
    )(x, sc, sh, w_lat, w_val, w_gate, w_brg)


def _conv_pw_kernel(prev_ref, cur_ref, wdw_ref, bdw_ref, lng_ref, lnb_ref, wpw_ref, o_ref,
                    ext_ref, conv_ref, act_ref, *, tm, cw, rh):
    i = pl.program_id(0)

    @pl.when(pl.program_id(1) == 0)
    def _():
        d = cur_ref.shape[1]
        halo = prev_ref[...].astype(F32)
        ext_ref[0:CONV_HALO, :] = jnp.where(i > 0, halo, 0.0)
        ext_ref[CONV_HALO:CONV_HALO + tm, :] = cur_ref[...].astype(F32)
        first = CONV_HALO - (CONV_WIDTH - 1)

        def chan_chunk(ci, carry):
            cs = pl.ds(pl.multiple_of(ci * cw, cw), cw)
            for r0 in range(0, tm, rh):
                partial = []
                for res in range(SUBLANES):
                    taps = [k for k in range(CONV_WIDTH) if (first + k) % SUBLANES == res]
                    lo = min(first + k for k in taps) - res
                    hi = max(first + k for k in taps) - res
                    shifted = ext_ref[r0 + lo + res:r0 + hi + res + rh, cs]
                    acc = None
                    for k in taps:
                        off = first + k - res - lo
                        prod = wdw_ref[k:k + 1, cs] * shifted[off:off + rh]
                        acc = prod if acc is None else acc + prod
                    partial.append(acc)
                while len(partial) > 1:
                    partial = [a + b for a, b in zip(partial[0::2], partial[1::2])]
                conv_ref[r0:r0 + rh, cs] = partial[0]
            return carry

        lax.fori_loop(0, d // cw, chan_chunk, 0)
        u = conv_ref[...] + bdw_ref[...]
        u = _layer_norm_rows(u) * lng_ref[...] + lnb_ref[...]
        act_ref[...] = (u * _sigmoid(u)).astype(BF16)

    o_ref[...] = jnp.dot(act_ref[...], wpw_ref[...], preferred_element_type=F32).astype(o_ref.dtype)


def _conv_pw(u, w_dw, b_dw, ln_g, ln_b, w_pw, *, tm=256, tn=2048, cw=256, rh=64):
    s, d = u.shape
    n = w_pw.shape[1]
    ratio = tm // CONV_HALO
    vec = pl.BlockSpec((1, d), lambda i, j: (0, 0))
    return pl.pallas_call(
        functools.partial(_conv_pw_kernel, tm=tm, cw=cw, rh=rh),
        grid=(s // tm, n // tn),
        in_specs=[
            pl.BlockSpec((CONV_HALO, d), lambda i, j: (jnp.maximum(i * ratio - 1, 0), 0)),
            pl.BlockSpec((tm, d), lambda i, j: (i, 0)),
            pl.BlockSpec((CONV_WIDTH, d), lambda i, j: (0, 0)),
            vec, vec, vec,
            pl.BlockSpec((d, tn), lambda i, j: (0, j)),
        ],
        out_specs=pl.BlockSpec((tm, tn), lambda i, j: (i, j)),
        out_shape=jax.ShapeDtypeStruct((s, n), BF16),
        scratch_shapes=[
            pltpu.VMEM((CONV_HALO + tm, d), F32),
            pltpu.VMEM((tm, d), F32),
            pltpu.VMEM((tm, d), BF16),
        ],
        compiler_params=_params("parallel", "arbitrary"),
        name="conv_pw",
    )(u, u, w_dw, b_dw, ln_g, ln_b, w_pw)


def _rms_rows(x, g):
    return x * lax.rsqrt(jnp.mean(x * x, axis=-1, keepdims=True) + RMS_EPS) * g


_NT = (((1,), (1,)), ((), ()))


def _qproj_kernel(cq_ref, g_ref, wt_ref, wrot_t_ref, cos_t_ref, sin_t_ref, o_ref, *, scale):
    cn = _rms_rows(cq_ref[...], g_ref[...]).astype(BF16)
    main_t = lax.dot_general(wt_ref[...], cn, _NT, preferred_element_type=F32)
    rot_t = lax.dot_general(wrot_t_ref[...], cn, _NT, preferred_element_type=F32)
    cos_t = cos_t_ref[...]
    sin_t = sin_t_ref[...]
    n_nope = N_HEADS * QK_NOPE_DIM
    for h in range(N_HEADS):
        nope = main_t[h * QK_NOPE_DIM:(h + 1) * QK_NOPE_DIM]
        r0 = h * QK_ROPE_DIM
        rope = main_t[n_nope + r0:n_nope + r0 + QK_ROPE_DIM] * cos_t + rot_t[r0:r0 + QK_ROPE_DIM] * sin_t
        o_ref[h, 0:QK_NOPE_DIM, :] = (nope * scale).astype(BF16)
        o_ref[h, QK_NOPE_DIM:QK_HEAD_DIM, :] = (rope * scale).astype(BF16)


def _qproj(lat, g_q, wq_t, wq_rot_t, cos_t, sin_t, *, tm=256):
    s = lat.shape[0]
    scale = QK_HEAD_DIM ** -0.5 * 1.4426950408889634
    return pl.pallas_call(
        functools.partial(_qproj_kernel, scale=scale),
        grid=(s // tm,),
        in_specs=[
            pl.BlockSpec((tm, Q_LORA), lambda i: (i, 0)),
            pl.BlockSpec((1, Q_LORA), lambda i: (0, 0)),
            pl.BlockSpec(wq_t.shape, lambda i: (0, 0)),
            pl.BlockSpec(wq_rot_t.shape, lambda i: (0, 0)),
            pl.BlockSpec((QK_ROPE_DIM, tm), lambda i: (0, i)),
            pl.BlockSpec((QK_ROPE_DIM, tm), lambda i: (0, i)),
        ],
        out_specs=pl.BlockSpec((N_HEADS, QK_HEAD_DIM, tm), lambda i: (0, 0, i)),
        out_shape=jax.ShapeDtypeStruct((N_HEADS, QK_HEAD_DIM, s), BF16),
        compiler_params=_params("parallel"),
        name="q_proj",
    )(lat, g_q, wq_t, wq_rot_t, cos_t, sin_t)


def _kvproj_kernel(ckv_ref, kr_ref, g_ref, wk_ref, wv_t_ref, cos_ref, sin_ref, k_ref, vt_ref):
    cn = _rms_rows(ckv_ref[...], g_ref[...]).astype(BF16)
    kr = kr_ref[:, 0:QK_ROPE_DIM]
    kr_rot = kr_ref[:, QK_ROPE_DIM:2 * QK_ROPE_DIM]
    krope = (kr * cos_ref[...] + kr_rot * sin_ref[...]).astype(BF16)
    k_all = jnp.dot(cn, wk_ref[...], preferred_element_type=F32)
    v_t = lax.dot_general(wv_t_ref[...], cn, _NT, preferred_element_type=F32)
    for h in range(N_HEADS):
        k_ref[h, :, 0:QK_NOPE_DIM] = k_all[:, h * QK_NOPE_DIM:(h + 1) * QK_NOPE_DIM].astype(BF16)
        k_ref[h, :, QK_NOPE_DIM:QK_HEAD_DIM] = krope
        vt_ref[h] = v_t[h * V_HEAD_DIM:(h + 1) * V_HEAD_DIM].astype(BF16)


def _kvproj(lat, g_kv, wk, wv_t, cos, sin, *, tm=256):
    s = lat.shape[0]
    kr_block = (Q_LORA + KV_LORA) // (2 * QK_ROPE_DIM)
    return pl.pallas_call(
        _kvproj_kernel,
        grid=(s // tm,),
        in_specs=[
            pl.BlockSpec((tm, KV_LORA), lambda i: (i, Q_LORA // KV_LORA)),
            pl.BlockSpec((tm, 2 * QK_ROPE_DIM), lambda i: (i, kr_block)),
            pl.BlockSpec((1, KV_LORA), lambda i: (0, 0)),
            pl.BlockSpec(wk.shape, lambda i: (0, 0)),
            pl.BlockSpec(wv_t.shape, lambda i: (0, 0)),
            pl.BlockSpec((tm, QK_ROPE_DIM), lambda i: (i, 0)),
            pl.BlockSpec((tm, QK_ROPE_DIM), lambda i: (i, 0)),
        ],
        out_specs=[
            pl.BlockSpec((N_HEADS, tm, QK_HEAD_DIM), lambda i: (0, i, 0)),
            pl.BlockSpec((N_HEADS, V_HEAD_DIM, tm), lambda i: (0, 0, i)),
        ],
        out_shape=[
            jax.ShapeDtypeStruct((N_HEADS, s, QK_HEAD_DIM), BF16),
            jax.ShapeDtypeStruct((N_HEADS, V_HEAD_DIM, s), BF16),
        ],
        compiler_params=_params("parallel"),
        name="kv_proj",
    )(lat, lat, g_kv, wk, wv_t, cos, sin)


def _flash_kernel(qt_ref, k_ref, vt_ref, o_ref, sa_ref, sb_ref, m_ref, l_ref, acc_ref, *, tq, tk):
    qi = pl.program_id(1)
    qt = qt_ref[0]

    def scores(j, s_ref):
        start = pl.multiple_of(j * tk, tk)
        s_ref[...] = jnp.dot(k_ref[0, pl.ds(start, tk), :], qt, preferred_element_type=F32)

    def consume(j, s_ref, mask):
        s = s_ref[...]
        if mask is not None:
            s = jnp.where(mask, s, -jnp.inf)
        m_prev = m_ref[...]
        m_new = jnp.maximum(m_prev, jnp.max(s, axis=0, keepdims=True))
        alpha = jnp.exp2(m_prev - m_new)
        p = jnp.exp2(s - m_new)
        l_ref[...] = alpha * l_ref[...] + jnp.sum(p, axis=0, keepdims=True)
        start = pl.multiple_of(j * tk, tk)
        pv = jnp.dot(vt_ref[0, :, pl.ds(start, tk)], p.astype(BF16), preferred_element_type=F32)
        acc_ref[...] = alpha * acc_ref[...] + pv
        m_ref[...] = m_new

    m_ref[...] = jnp.full(m_ref.shape, -jnp.inf, F32)
    l_ref[...] = jnp.zeros(l_ref.shape, F32)
    acc_ref[...] = jnp.zeros(acc_ref.shape, F32)

    scores(0, sa_ref)

    def pair(tt, carry):
        j = 2 * tt
        scores(j + 1, sb_ref)
        consume(j, sa_ref, None)
        scores(j + 2, sa_ref)
        consume(j + 1, sb_ref, None)
        return carry

    lax.fori_loop(0, qi, pair, 0)

    key_chunk = lax.broadcasted_iota(jnp.int32, (tk, tq), 0) // CHUNK
    query_chunk = lax.broadcasted_iota(jnp.int32, (tk, tq), 1) // CHUNK
    j_diag = 2 * qi
    scores(j_diag + 1, sb_ref)
    consume(j_diag, sa_ref, key_chunk <= query_chunk)
    consume(j_diag + 1, sb_ref, key_chunk + tk // CHUNK <= query_chunk)

    out_t = acc_ref[...] / l_ref[...]
    o_ref[...] = out_t.T.astype(o_ref.dtype)


def _flash(q_t, k, v_t, *, tq=1024, tk=512):
    assert tq == 2 * tk
    h, _, s = q_t.shape
    return pl.pallas_call(
        functools.partial(_flash_kernel, tq=tq, tk=tk),
        grid=(h, s // tq),
        in_specs=[
            pl.BlockSpec((1, QK_HEAD_DIM, tq), lambda hh, i: (hh, 0, i)),
            pl.BlockSpec((1, s, QK_HEAD_DIM), lambda hh, i: (hh, 0, 0)),
            pl.BlockSpec((1, V_HEAD_DIM, s), lambda hh, i: (hh, 0, 0)),
        ],
        out_specs=pl.BlockSpec((tq, V_HEAD_DIM), lambda hh, i: (i, hh)),
        out_shape=jax.ShapeDtypeStruct((s, h * V_HEAD_DIM), BF16),
        scratch_shapes=[
            pltpu.VMEM((tk, tq), F32),
            pltpu.VMEM((tk, tq), F32),
            pltpu.VMEM((1, tq), F32),
            pltpu.VMEM((1, tq), F32),
            pltpu.VMEM((V_HEAD_DIM, tq), F32),
        ],
        compiler_params=_params("parallel", "arbitrary"),
        name="flash_attn",
    )(q_t, k, v_t)


def _wo_mix_kernel(o_ref, w_ref, gc_ref, gm_ref, yc_ref, out_ref):
    y_mla = jnp.dot(o_ref[...], w_ref[...], preferred_element_type=F32)
    mixed = gc_ref[...].astype(F32) * yc_ref[...].astype(F32) + gm_ref[...].astype(F32) * y_mla
    out_ref[...] = mixed.astype(out_ref.dtype)


def _wo_mix(o, w_o, gates, y_conv, *, tm=1024, tn=512):
    s, d = o.shape
    n = w_o.shape[1]
    nb = n // tn
    return pl.pallas_call(
        _wo_mix_kernel,
        grid=(s // tm, nb),
        in_specs=[
            pl.BlockSpec((tm, d), lambda i, j: (i, 0)),
            pl.BlockSpec((d, tn), lambda i, j: (0, j)),
            pl.BlockSpec((tm, tn), lambda i, j: (i, j)),
            pl.BlockSpec((tm, tn), lambda i, j: (i, j + nb)),
            pl.BlockSpec((tm, tn), lambda i, j: (i, j)),
        ],
        out_specs=pl.BlockSpec((tm, tn), lambda i, j: (i, j)),
        out_shape=jax.ShapeDtypeStruct((s, n), BF16),
        compiler_params=_params("parallel", "parallel"),
        name="wo_mix",
    )(o, w_o, gates, gates, y_conv)


def _resid_ln(x, f, gate, lng, lnb):
    y = DEEPNORM_ALPHA * x + (1.0 + gate) * f
    return _layer_norm_rows(y) * lng + lnb


def _mm_resid_ln_kernel(a_ref, w_ref, x_ref, gate_ref, lng_ref, lnb_ref, o_ref, acc_ref):
    kk = pl.program_id(1)

    @pl.when(kk == 0)
    def _():
        acc_ref[...] = jnp.zeros(acc_ref.shape, F32)

    acc_ref[...] += jnp.dot(a_ref[...], w_ref[...], preferred_element_type=F32)

    @pl.when(kk == pl.num_programs(1) - 1)
    def _():
        o_ref[...] = _resid_ln(x_ref[...], acc_ref[...], gate_ref[...], lng_ref[...], lnb_ref[...])


def _mm_resid_ln(a, w, x, gate, lng, lnb, *, tm=512, tk, name):
    s, kdim = a.shape
    d = w.shape[1]
    vec = pl.BlockSpec((1, d), lambda i, k: (0, 0))
    return pl.pallas_call(
        _mm_resid_ln_kernel,
        grid=(s // tm, kdim // tk),
        in_specs=[
            pl.BlockSpec((tm, tk), lambda i, k: (i, k)),
            pl.BlockSpec((tk, d), lambda i, k: (k, 0)),
            pl.BlockSpec((tm, d), lambda i, k: (i, 0)),
            vec, vec, vec,
        ],
        out_specs=pl.BlockSpec((tm, d), lambda i, k: (i, 0)),
        out_shape=jax.ShapeDtypeStruct((s, d), F32),
        scratch_shapes=[pltpu.VMEM((tm, d), F32)],
        compiler_params=_params("parallel", "arbitrary"),
        name=name,
    )(a, w, x, gate, lng, lnb)


def _router_kernel(x_ref, sc_ref, sh_ref, wr_ref, h_ref, comb_ref, sel_ref):
    h = _layer_norm_rows(x_ref[...]) * (1.0 + sc_ref[...]) + sh_ref[...]
    h_ref[...] = _pack_bf16_pair(h)
    logits = jnp.dot(h, wr_ref[...], preferred_element_type=F32, precision=lax.Precision.HIGHEST)
    lane = lax.broadcasted_iota(jnp.int32, logits.shape, 1)
    neg = -jnp.inf
    l1 = jnp.where(lane < N_EXPERTS, logits, neg)
    v1 = jnp.max(l1, axis=1, keepdims=True)
    i1 = jnp.min(jnp.where(l1 == v1, lane, LANES), axis=1, keepdims=True)
    l2 = jnp.where(lane == i1, neg, l1)
    v2 = jnp.max(l2, axis=1, keepdims=True)
    i2 = jnp.min(jnp.where(l2 == v2, lane, LANES), axis=1, keepdims=True)
    e2 = jnp.exp(v2 - v1)
    w1 = 1.0 / (1.0 + e2)
    w2 = e2 / (1.0 + e2)
    comb_ref[...] = jnp.where(lane == i1, w1, 0.0) + jnp.where(lane == i2, w2, 0.0)
    sel_ref[...] = jnp.where(lane == i1, 1.0, 0.0) + jnp.where(lane == i2, 2.0, 0.0)


def _router(x, sc, sh, wr_pad, *, tm=256):
    s, d = x.shape
    vec = pl.BlockSpec((1, d), lambda i: (0, 0))
    lane_blk = pl.BlockSpec((tm, LANES), lambda i: (i, 0))
    return pl.pallas_call(
        _router_kernel,
        grid=(s // tm,),
        in_specs=[pl.BlockSpec((tm, d), lambda i: (i, 0)), vec, vec,
                  pl.BlockSpec((d, LANES), lambda i: (0, 0))],
        out_specs=[pl.BlockSpec((tm, d // 2), lambda i: (i, 0)), lane_blk, lane_blk],
        out_shape=[jax.ShapeDtypeStruct((s, d // 2), jnp.uint32), jax.ShapeDtypeStruct((s, LANES), F32),
                   jax.ShapeDtypeStruct((s, LANES), F32)],
        compiler_params=_params("parallel"),
        name="router",
    )(x, sc, sh, wr_pad)


def _routing_plan(comb, sel, tm, n_tiles):
    sel8 = sel[:, :N_EXPERTS]
    comb8 = comb[:, :N_EXPERTS]
    chosen = (sel8 > 0).astype(jnp.int32)
    rank = jnp.cumsum(chosen, axis=0) - chosen
    counts = jnp.sum(chosen, axis=0)
    tiles_e = (counts + tm - 1) // tm
    tile_end = jnp.cumsum(tiles_e)
    row_start = (tile_end - tiles_e) * tm
    dest = row_start[None, :] + rank
    pos0 = jnp.sum(jnp.where(sel8 == 1.0, dest, 0), axis=1).astype(jnp.int32)
    pos1 = jnp.sum(jnp.where(sel8 == 2.0, dest, 0), axis=1).astype(jnp.int32)
    w0 = jnp.sum(jnp.where(sel8 == 1.0, comb8, 0.0), axis=1, keepdims=True)
    w1 = jnp.sum(jnp.where(sel8 == 2.0, comb8, 0.0), axis=1, keepdims=True)
    n_valid = tile_end[-1]
    tile_id = jnp.minimum(jnp.arange(n_tiles, dtype=jnp.int32), n_valid - 1)
    tile_expert = jnp.sum((tile_id[:, None] >= tile_end[None, :]).astype(jnp.int32), axis=1)
    meta = jnp.concatenate([tile_expert, n_valid[None]]).astype(jnp.int32)
    tok = jnp.arange(sel.shape[0], dtype=jnp.int32)
    src_token = jnp.zeros((n_tiles * tm,), jnp.int32).at[jnp.concatenate([pos0, pos1])].set(
        jnp.concatenate([tok, tok]), unique_indices=True)
    return pos0, pos1, w0, w1, meta, src_token


_HIGH_HALF = 0xFFFF0000


def _pack_bf16_pair(x):
    n = x.shape[1] // 2
    bits = lax.bitcast_convert_type(x.astype(BF16).astype(F32), jnp.uint32)
    return (bits[:, :n] >> 16) | (bits[:, n:] & jnp.uint32(_HIGH_HALF))


def _unpack_bf16_pair(w):
    lo = lax.bitcast_convert_type(w << 16, F32)
    hi = lax.bitcast_convert_type(w & jnp.uint32(_HIGH_HALF), F32)
    return lo, hi


def _row_copy(src_hbm, dst_hbm, src_row, dst_row, sem):
    return pltpu.make_async_copy(src_hbm.at[pl.ds(src_row, 1)], dst_hbm.at[pl.ds(dst_row, 1)], sem)


def _dispatch_kernel(meta_ref, src_ref, src_next_ref, h_hbm, o_ref, buf, sem, *, tb, n_tiles):
    i = pl.program_id(0)
    n_used = meta_ref[n_tiles]
    slot = i % 2

    def start_gather(idx_ref, dst_slot):
        def issue(q, carry):
            for lane in range(2):
                r = 2 * q + lane
                _row_copy(h_hbm, buf.at[dst_slot], idx_ref[0, 0, r], r, sem.at[dst_slot]).start(priority=lane)
            return carry

        lax.fori_loop(0, tb // 2, issue, 0)

    @pl.when(jnp.logical_and(i == 0, n_used > 0))
    def _():
        start_gather(src_ref, 0)

    @pl.when(i + 1 < n_used)
    def _():
        start_gather(src_next_ref, 1 - slot)

    @pl.when(i < n_used)
    def _():
        def drain(r, carry):
            _row_copy(h_hbm, buf.at[slot], 0, 0, sem.at[slot]).wait()
            return carry

        lax.fori_loop(0, tb, drain, 0)
        lo, hi = _unpack_bf16_pair(buf[slot])
        half = lo.shape[1]
        o_ref[:, :half] = lo.astype(o_ref.dtype)
        o_ref[:, half:] = hi.astype(o_ref.dtype)

    @pl.when(i >= n_used)
    def _():
        o_ref[...] = jnp.zeros(o_ref.shape, o_ref.dtype)


def _dispatch(meta, src_token, h, *, tb):
    n_rows = src_token.shape[0]
    dp = h.shape[1]
    d = 2 * dp
    n_tiles = n_rows // tb
    return pl.pallas_call(
        functools.partial(_dispatch_kernel, tb=tb, n_tiles=n_tiles),
        grid_spec=pltpu.PrefetchScalarGridSpec(
            num_scalar_prefetch=1,
            grid=(n_tiles,),
            in_specs=[
                pl.BlockSpec((1, 1, tb), lambda i, m: (i, 0, 0), memory_space=pltpu.SMEM),
                pl.BlockSpec((1, 1, tb), lambda i, m: (jnp.minimum(i + 1, n_tiles - 1), 0, 0),
                             memory_space=pltpu.SMEM),
                pl.BlockSpec(memory_space=pl.ANY),
            ],
            out_specs=pl.BlockSpec((tb, d), lambda i, m: (i, 0)),
            scratch_shapes=[pltpu.VMEM((2, tb, dp), jnp.uint32), pltpu.SemaphoreType.DMA((2,))],
        ),
        out_shape=jax.ShapeDtypeStruct((n_rows, d), BF16),
        compiler_params=_params("arbitrary"),
        name="moe_dispatch",
    )(meta, src_token.reshape(n_tiles, 1, tb), src_token.reshape(n_tiles, 1, tb), h)


def _expert_swiglu_kernel(meta_ref, x_ref, wg_ref, wu_ref, wd_ref, o_ref, wd_bf_ref, wg_bf, wu_bf, *, n_tiles):
    i = pl.program_id(1)

    wd_bf_ref[...] = wd_ref[0].astype(BF16)

    @pl.when(jnp.logical_or(i == 0, meta_ref[i] != meta_ref[jnp.maximum(i - 1, 0)]))
    def _():
        wg_bf[...] = wg_ref[0, 0].astype(BF16)
        wu_bf[...] = wu_ref[0, 0].astype(BF16)

    @pl.when(i < meta_ref[n_tiles])
    def _():
        x = x_ref[...]
        a = jnp.dot(x, wg_bf[...], preferred_element_type=F32)
        b = jnp.dot(x, wu_bf[...], preferred_element_type=F32)
        o_ref[...] = (a * _sigmoid(a) * b).astype(o_ref.dtype)

    @pl.when(i >= meta_ref[n_tiles])
    def _():
        o_ref[...] = jnp.zeros(o_ref.shape, o_ref.dtype)


def _expert_swiglu(meta, xs, wg, wu, wd, *, layer, tm, tn=1024):
    r, d = xs.shape
    n_layers, n_exp, _, f = wg.shape
    n_tiles = r // tm
    n_steps = (f // tn) * n_tiles
    wd_rows = n_exp * f
    rb = next(c for c in (256, 512, 1024, 2048, 4096) if wd_rows % c == 0 and wd_rows // c <= n_steps)
    n_wd_blocks = wd_rows // rb
    w_spec = pl.BlockSpec((1, 1, d, tn), lambda j, i, m: (layer, m[i], 0, j))

    def row_tile(i, m):
        return jnp.minimum(i, m[n_tiles] - 1)

    def wd_block(j, i):
        return jnp.minimum(j * n_tiles + i, n_wd_blocks - 1)

    return pl.pallas_call(
        functools.partial(_expert_swiglu_kernel, n_tiles=n_tiles),
        grid_spec=pltpu.PrefetchScalarGridSpec(
            num_scalar_prefetch=1,
            grid=(f // tn, n_tiles),
            in_specs=[
                pl.BlockSpec((tm, d), lambda j, i, m: (row_tile(i, m), 0)),
                w_spec,
                w_spec,
                pl.BlockSpec((1, rb, d), lambda j, i, m: (layer, wd_block(j, i), 0)),
            ],
            out_specs=[
                pl.BlockSpec((tm, tn), lambda j, i, m: (i, j)),
                pl.BlockSpec((rb, d), lambda j, i, m: (wd_block(j, i), 0)),
            ],
            scratch_shapes=[pltpu.VMEM((d, tn), BF16), pltpu.VMEM((d, tn), BF16)],
        ),
        out_shape=[jax.ShapeDtypeStruct((r, f), BF16), jax.ShapeDtypeStruct((wd_rows, d), BF16)],
        compiler_params=pltpu.CompilerParams(dimension_semantics=("arbitrary", "arbitrary"),
                                             vmem_limit_bytes=EXPERT_VMEM_LIMIT),
        name="expert_swiglu",
    )(meta, xs, wg, wu, wd.reshape(n_layers, wd_rows, d))


def _expert_down_kernel(meta_ref, a_ref, w_ref, o_ref, acc_ref, *, n_tiles):
    i = pl.program_id(0)
    kk = pl.program_id(1)

    @pl.when(i < meta_ref[n_tiles])
    def _():
        @pl.when(kk == 0)
        def _():
            acc_ref[...] = jnp.zeros(acc_ref.shape, F32)

        acc_ref[...] += jnp.dot(a_ref[...], w_ref[0], preferred_element_type=F32)

        @pl.when(kk == pl.num_programs(1) - 1)
        def _():
            o_ref[...] = _pack_bf16_pair(acc_ref[...])

    @pl.when(jnp.logical_and(i >= meta_ref[n_tiles], kk == 0))
    def _():
        o_ref[...] = jnp.zeros(o_ref.shape, o_ref.dtype)


def _expert_down(meta, act, wd, *, tm, tk=1024):
    r, f = act.shape
    d = wd.shape[2]
    n_tiles = r // tm
    nk = f // tk

    def used(i, m):
        return i < m[n_tiles]

    def row_tile(i, m):
        return jnp.minimum(i, m[n_tiles] - 1)

    def k_tile(i, k, m):
        return jnp.where(used(i, m), k, nk - 1)

    return pl.pallas_call(
        functools.partial(_expert_down_kernel, n_tiles=n_tiles),
        grid_spec=pltpu.PrefetchScalarGridSpec(
            num_scalar_prefetch=1,
            grid=(n_tiles, nk),
            in_specs=[
                pl.BlockSpec((tm, tk), lambda i, k, m: (row_tile(i, m), k_tile(i, k, m))),
                pl.BlockSpec((1, tk, d), lambda i, k, m: (m[i], k_tile(i, k, m), 0)),
            ],
            out_specs=pl.BlockSpec((tm, d // 2), lambda i, k, m: (i, 0)),
            scratch_shapes=[pltpu.VMEM((tm, d), F32)],
        ),
        out_shape=jax.ShapeDtypeStruct((r, d // 2), jnp.uint32),
        compiler_params=_params("arbitrary", "arbitrary"),
        name="expert_down",
    )(meta, act, wd)


def _combine_ln_kernel(pos0_ref, pos1_ref, pos0_next_ref, pos1_next_ref, ys_hbm, w0_ref, w1_ref, x_ref,
                       gate_ref, lng_ref, lnb_ref, o_ref, a_buf, b_buf, sem, *, tb):
    i = pl.program_id(0)
    slot = i % 2

    def start_gather(p0_ref, p1_ref, dst_slot):
        def issue(r, carry):
            _row_copy(ys_hbm, a_buf.at[dst_slot], p0_ref[0, 0, r], r, sem.at[dst_slot]).start(priority=0)
            _row_copy(ys_hbm, b_buf.at[dst_slot], p1_ref[0, 0, r], r, sem.at[dst_slot]).start(priority=1)
            return carry

        lax.fori_loop(0, tb, issue, 0)

    @pl.when(i == 0)
    def _():
        start_gather(pos0_ref, pos1_ref, 0)

    @pl.when(i + 1 < pl.num_programs(0))
    def _():
        start_gather(pos0_next_ref, pos1_next_ref, 1 - slot)

    def drain(r, carry):
        _row_copy(ys_hbm, a_buf.at[slot], 0, 0, sem.at[slot]).wait()
        _row_copy(ys_hbm, b_buf.at[slot], 0, 0, sem.at[slot]).wait()
        return carry

    lax.fori_loop(0, tb, drain, 0)
    a_lo, a_hi = _unpack_bf16_pair(a_buf[slot])
    b_lo, b_hi = _unpack_bf16_pair(b_buf[slot])
    w0 = w0_ref[...]
    w1 = w1_ref[...]
    f = jnp.concatenate([w0 * a_lo + w1 * b_lo, w0 * a_hi + w1 * b_hi], axis=1)
    o_ref[...] = _resid_ln(x_ref[...], f, gate_ref[...], lng_ref[...], lnb_ref[...])


def _combine_ln(ys, pos0, pos1, w0, w1, x, gate, lng, lnb, *, tb=256):
    s, d = x.shape
    n_blk = s // tb
    idx_blk = pl.BlockSpec((1, 1, tb), lambda i: (i, 0, 0), memory_space=pltpu.SMEM)
    idx_next = pl.BlockSpec((1, 1, tb), lambda i: (jnp.minimum(i + 1, n_blk - 1), 0, 0), memory_space=pltpu.SMEM)
    vec = pl.BlockSpec((1, d), lambda i: (0, 0))
    col = pl.BlockSpec((tb, 1), lambda i: (i, 0))
    blk = pl.BlockSpec((tb, d), lambda i: (i, 0))
    p0 = pos0.reshape(n_blk, 1, tb)
    p1 = pos1.reshape(n_blk, 1, tb)
    return pl.pallas_call(
        functools.partial(_combine_ln_kernel, tb=tb),
        grid=(n_blk,),
        in_specs=[idx_blk, idx_blk, idx_next, idx_next, pl.BlockSpec(memory_space=pl.ANY),
                  col, col, blk, vec, vec, vec],
        out_specs=blk,
        out_shape=jax.ShapeDtypeStruct((s, d), F32),
        scratch_shapes=[pltpu.VMEM((2, tb, d // 2), jnp.uint32), pltpu.VMEM((2, tb, d // 2), jnp.uint32),
                        pltpu.SemaphoreType.DMA((2,))],
        compiler_params=_params("arbitrary"),
        name="moe_combine_ln",
    )(p0, p1, p0, p1, ys, w0, w1, x, gate, lng, lnb)


def _rotate_half_cols(w):
    half = w.shape[-1] // 2
    return jnp.concatenate([-w[..., half:], w[..., :half]], axis=-1)


def kernel(x, c, positions, w_ada, b_ada, w_in, g_q, w_uq, g_kv, w_ukv, w_o, w_dw, b_dw, conv_ln_g, conv_ln_b, w_pw, w_out, ln1_g, ln1_b, ln2_g, ln2_b, w_ff_gate, w_ff_up, w_ff_down, w_router, w_e_gate, w_e_up, w_e_down):
    assert x.shape == (1, SEQ, D_MODEL) and c.shape == (1, D_MODEL)
    d = D_MODEL
    xs = x.reshape(SEQ, d)

    mod = _ada(c.reshape(d, 1), w_ada, b_ada).reshape(DEPTH, 6, 1, d)

    inv_freq = ROPE_THETA ** (-jnp.arange(0, QK_ROPE_DIM, 2, dtype=F32) / QK_ROPE_DIM)
    inv_freq2 = jnp.concatenate([inv_freq, inv_freq])
    cos, sin, cos_t, sin_t = _rope_tables(positions.reshape(SEQ), inv_freq2)

    row = lambda v: v.reshape(1, -1)

    for l in range(DEPTH):
        sh1, sc1, g1, sh2, sc2, g2 = (mod[l, t] for t in range(6))
        wl = w_in[l]
        o_kr = Q_LORA + KV_LORA
        o_val = o_kr + QK_ROPE_DIM
        w_kr = wl[:, o_kr:o_val]
        w_lat = jnp.concatenate([wl[:, :o_val], _rotate_half_cols(w_kr)], axis=1).astype(BF16)
        w_val = wl[:, o_val:o_val + d].astype(BF16)
        w_gate = wl[:, o_val + d:o_val + 2 * d].astype(BF16)
        w_brg = wl[:, o_val + 2 * d:].astype(BF16)

        lat, u, gates = _in_proj(xs, sc1, sh1, w_lat, w_val, w_gate, w_brg)

        y_conv = _conv_pw(u, w_dw[l], row(b_dw[l]), row(conv_ln_g[l]), row(conv_ln_b[l]), w_pw[l].astype(BF16))

        wq = w_uq[l].reshape(Q_LORA, N_HEADS, QK_HEAD_DIM)
        wq_rope = wq[..., QK_NOPE_DIM:]
        wq_t = jnp.concatenate([wq[..., :QK_NOPE_DIM].reshape(Q_LORA, -1), wq_rope.reshape(Q_LORA, -1)], axis=1).T
        wq_rot_t = _rotate_half_cols(wq_rope).reshape(Q_LORA, -1).T
        wkv = w_ukv[l].reshape(KV_LORA, N_HEADS, QK_NOPE_DIM + V_HEAD_DIM)
        wk = wkv[..., :QK_NOPE_DIM].reshape(KV_LORA, -1)
        wv_t = wkv[..., QK_NOPE_DIM:].reshape(KV_LORA, -1).T
        q_t = _qproj(lat, row(g_q[l]), wq_t.astype(BF16), wq_rot_t.astype(BF16), cos_t, sin_t)
        k, v_t = _kvproj(lat, row(g_kv[l]), wk.astype(BF16), wv_t.astype(BF16), cos, sin)
        o = _flash(q_t, k, v_t)
        mixed = _wo_mix(o, w_o[l].astype(BF16), gates, y_conv)
        xs = _mm_resid_ln(mixed, w_out[l].astype(BF16), xs, g1, row(ln1_g[l]), row(ln1_b[l]), tk=d, name="out_proj_ln")

        i = l // 2
        if l % 2 == 0:
            act = _ffn_swiglu(xs, sc2, sh2, w_ff_gate[i].astype(BF16), w_ff_up[i].astype(BF16))
            xs = _mm_resid_ln(act, w_ff_down[i].astype(BF16), xs, g2, row(ln2_g[l]), row(ln2_b[l]), tk=D_FF_DENSE // 4, name="ffn_down_ln")
        else:
            wr_pad = jnp.zeros((d, LANES), F32).at[:, :N_EXPERTS].set(w_router[i])
            h2, comb, sel = _router(xs, sc2, sh2, wr_pad)
            tm_e = EXPERT_ROW_TILE
            n_tiles = (2 * SEQ) // tm_e + N_EXPERTS
            pos0, pos1, w0, w1, meta, src_token = _routing_plan(comb, sel, tm_e, n_tiles)
            x_sorted = _dispatch(meta, src_token, h2, tb=tm_e)
            act, wd_bf = _expert_swiglu(meta, x_sorted, w_e_gate, w_e_up, w_e_down, layer=i, tm=tm_e)
            y_sorted = _expert_down(meta, act, wd_bf.reshape(N_EXPERTS, D_FF_EXPERT, d), tm=tm_e)
            xs = _combine_ln(y_sorted, pos0, pos1, w0, w1, xs, g2, row(ln2_g[l]), row(ln2_b[l]))

    return xs.reshape(1, SEQ, D_MODEL)
```

```python
import functools

import jax
import jax.numpy as jnp
from jax import lax
from jax.experimental import pallas as pl
from jax.experimental.pallas import tpu as pltpu

F32 = jnp.float32
BF16 = jnp.bfloat16

D_MODEL = 2048
SEQ = 8192
DEPTH = 2
CHUNK = 64
N_HEADS = 16
QK_NOPE_DIM = 128
QK_ROPE_DIM = 64
V_HEAD_DIM = 128
QK_HEAD_DIM = QK_NOPE_DIM + QK_ROPE_DIM
Q_LORA = 512
KV_LORA = 512
ROPE_THETA = 10000.0
CONV_WIDTH = 31
D_FF_DENSE = 5632
N_EXPERTS = 8
D_FF_EXPERT = 7168
DEEPNORM_ALPHA = (2.0 * DEPTH) ** 0.25
LN_EPS = 1e-5
RMS_EPS = 1e-6
LATENT_COLS = Q_LORA + KV_LORA + 2 * QK_ROPE_DIM
CONV_HALO = 32
LANES = 128
SUBLANES = 8
EXPERT_ROW_TILE = 512

VMEM_LIMIT = 48 * 1024 * 1024
EXPERT_VMEM_LIMIT = 58 * 1024 * 1024


def _params(*sem):
    return pltpu.CompilerParams(dimension_semantics=sem, vmem_limit_bytes=VMEM_LIMIT)


def _layer_norm_rows(x):
    mu = jnp.mean(x, axis=-1, keepdims=True)
    xc = x - mu
    var = jnp.mean(xc * xc, axis=-1, keepdims=True)
    return xc * lax.rsqrt(var + LN_EPS)


def _sigmoid(x):
    return 1.0 / (1.0 + jnp.exp(-x))


ADA_STREAMS = 4


def _ada_kernel(c_ref, *rest, tk):
    w_refs, (b_ref, o_ref) = rest[:ADA_STREAMS], rest[ADA_STREAMS:]
    k = pl.program_id(1)

    @pl.when(k == 0)
    def _():
        o_ref[0] = b_ref[0]

    part = None
    for q, w_ref in enumerate(w_refs):
        c = c_ref[pl.ds(pl.multiple_of((k * ADA_STREAMS + q) * tk, tk), tk), :]
        term = jnp.sum(c * _sigmoid(c) * w_ref[0], axis=0, keepdims=True)
        part = term if part is None else part + term
    o_ref[0] += part


def _ada(c_col, w_ada, b_ada, tk=64):
    depth, d, n = w_ada.shape
    w_specs = [pl.BlockSpec((1, tk, n), functools.partial(lambda l, k, q: (l, k * ADA_STREAMS + q, 0), q=q))
               for q in range(ADA_STREAMS)]
    return pl.pallas_call(
        functools.partial(_ada_kernel, tk=tk),
        grid=(depth, d // (tk * ADA_STREAMS)),
        in_specs=[pl.BlockSpec((d, 1), lambda l, k: (0, 0))] + w_specs
        + [pl.BlockSpec((1, 1, n), lambda l, k: (l, 0, 0))],
        out_specs=pl.BlockSpec((1, 1, n), lambda l, k: (l, 0, 0)),
        out_shape=jax.ShapeDtypeStruct((depth, 1, n), F32),
        compiler_params=_params("parallel", "arbitrary"),
        name="ada_mod",
    )(c_col, *([w_ada] * ADA_STREAMS), b_ada.reshape(depth, 1, n))


def _rope_kernel(pos_col_ref, pos_row_ref, inv_row_ref, inv_col_ref, cos_ref, sin_ref, cos_t_ref, sin_t_ref):
    ang = pos_col_ref[...].astype(F32) * inv_row_ref[...]
    cos_ref[...] = jnp.cos(ang)
    sin_ref[...] = jnp.sin(ang)
    ang_t = inv_col_ref[...] * pos_row_ref[...].astype(F32)
    cos_t_ref[...] = jnp.cos(ang_t)
    sin_t_ref[...] = jnp.sin(ang_t)


def _rope_tables(positions, inv_freq2, ts=512):
    s = positions.shape[0]
    r = inv_freq2.shape[0]
    return pl.pallas_call(
        _rope_kernel,
        grid=(s // ts,),
        in_specs=[
            pl.BlockSpec((ts, 1), lambda i: (i, 0)),
            pl.BlockSpec((1, ts), lambda i: (0, i)),
            pl.BlockSpec((1, r), lambda i: (0, 0)),
            pl.BlockSpec((r, 1), lambda i: (0, 0)),
        ],
        out_specs=[pl.BlockSpec((ts, r), lambda i: (i, 0))] * 2 + [pl.BlockSpec((r, ts), lambda i: (0, i))] * 2,
        out_shape=[jax.ShapeDtypeStruct((s, r), F32)] * 2 + [jax.ShapeDtypeStruct((r, s), F32)] * 2,
        compiler_params=_params("parallel"),
        name="rope_tables",
    )(positions.reshape(s, 1), positions.reshape(1, s), inv_freq2.reshape(1, r), inv_freq2.reshape(r, 1))


def _ffn_swiglu_kernel(x_ref, sc_ref, sh_ref, wg_ref, wu_ref, o_ref, h_ref):
    @pl.when(pl.program_id(1) == 0)
    def _():
        h = _layer_norm_rows(x_ref[...]) * (1.0 + sc_ref[...]) + sh_ref[...]
        h_ref[...] = h.astype(BF16)

    h = h_ref[...]
    a = jnp.dot(h, wg_ref[...], preferred_element_type=F32)
    b = jnp.dot(h, wu_ref[...], preferred_element_type=F32)
    o_ref[...] = (a * _sigmoid(a) * b).astype(o_ref.dtype)


def _ffn_swiglu(x, sc, sh, wg, wu, *, tm=1024, tn=512):
    s, d = x.shape
    n = wg.shape[1]
    w_spec = pl.BlockSpec((d, tn), lambda i, j: (0, j))
    vec = pl.BlockSpec((1, d), lambda i, j: (0, 0))
    return pl.pallas_call(
        _ffn_swiglu_kernel,
        grid=(s // tm, n // tn),
        in_specs=[pl.BlockSpec((tm, d), lambda i, j: (i, 0)), vec, vec, w_spec, w_spec],
        out_specs=pl.BlockSpec((tm, tn), lambda i, j: (i, j)),
        out_shape=jax.ShapeDtypeStruct((s, n), BF16),
        scratch_shapes=[pltpu.VMEM((tm, d), BF16)],
        compiler_params=_params("parallel", "arbitrary"),
        name="ffn_swiglu",
    )(x, sc, sh, wg, wu)


def _in_proj_kernel(x_ref, sc_ref, sh_ref, wlat_ref, wval_ref, wgate_ref, wbrg_ref, lat_ref, u_ref, g_ref,
                    h_ref, *, n_lat, n_glu):
    j = pl.program_id(1)

    @pl.when(j == 0)
    def _():
        h = _layer_norm_rows(x_ref[...]) * (1.0 + sc_ref[...]) + sh_ref[...]
        h_ref[...] = h.astype(BF16)

    @pl.when(j < n_lat)
    def _():
        lat_ref[...] = jnp.dot(h_ref[...], wlat_ref[...], preferred_element_type=F32)

    @pl.when(jnp.logical_and(j >= n_lat, j < n_lat + n_glu))
    def _():
        h = h_ref[...]
        a = jnp.dot(h, wval_ref[...], preferred_element_type=F32)
        b = jnp.dot(h, wgate_ref[...], preferred_element_type=F32)
        u_ref[...] = (a * _sigmoid(b)).astype(u_ref.dtype)

    @pl.when(j >= n_lat + n_glu)
    def _():
        a = jnp.dot(h_ref[...], wbrg_ref[...], preferred_element_type=F32)
        g_ref[...] = _sigmoid(a).astype(g_ref.dtype)


def _in_proj(x, sc, sh, w_lat, w_val, w_gate, w_brg, *, tm=1024, tn=512, tn_lat=LATENT_COLS // 3):
    s, d = x.shape
    n_lat = w_lat.shape[1] // tn_lat
    n_glu = w_val.shape[1] // tn
    n_brg = w_brg.shape[1] // tn

    def lat_col(j):
        return jnp.minimum(j, n_lat - 1)

    def glu_col(j):
        return jnp.clip(j - n_lat, 0, n_glu - 1)

    def brg_col(j):
        return jnp.clip(j - n_lat - n_glu, 0, n_brg - 1)

    vec = pl.BlockSpec((1, d), lambda i, j: (0, 0))
    return pl.pallas_call(
        functools.partial(_in_proj_kernel, n_lat=n_lat, n_glu=n_glu),
        grid=(s // tm, n_lat + n_glu + n_brg),
        in_specs=[
            pl.BlockSpec((tm, d), lambda i, j: (i, 0)), vec, vec,
            pl.BlockSpec((d, tn_lat), lambda i, j: (0, lat_col(j))),
            pl.BlockSpec((d, tn), lambda i, j: (0, glu_col(j))),
            pl.BlockSpec((d, tn), lambda i, j: (0, glu_col(j))),
            pl.BlockSpec((d, tn), lambda i, j: (0, brg_col(j))),
        ],
        out_specs=[
            pl.BlockSpec((tm, tn_lat), lambda i, j: (i, lat_col(j))),
            pl.BlockSpec((tm, tn), lambda i, j: (i, glu_col(j))),
            pl.BlockSpec((tm, tn), lambda i, j: (i, brg_col(j))),
        ],
        out_shape=[
            jax.ShapeDtypeStruct((s, w_lat.shape[1]), F32),
            jax.ShapeDtypeStruct((s, w_val.shape[1]), BF16),
            jax.ShapeDtypeStruct((s, w_brg.shape[1]), BF16),
        ],
        scratch_shapes=[pltpu.VMEM((tm, d), BF16)],
        compiler_params=_params("parallel", "arbitrary"),
        name="in_proj",
    )(x, sc, sh, w_lat, w_val, w_gate, w_brg)


def _conv_pw_kernel(prev_ref, cur_ref, wdw_ref, bdw_ref, lng_ref, lnb_ref, wpw_ref, o_ref,
                    ext_ref, conv_ref, act_ref, *, tm, cw, rh):
    i = pl.program_id(0)

    @pl.when(pl.program_id(1) == 0)
    def _():
        d = cur_ref.shape[1]
        halo = prev_ref[...].astype(F32)
        ext_ref[0:CONV_HALO, :] = jnp.where(i > 0, halo, 0.0)
        ext_ref[CONV_HALO:CONV_HALO + tm, :] = cur_ref[...].astype(F32)
        first = CONV_HALO - (CONV_WIDTH - 1)

        def chan_chunk(ci, carry):
            cs = pl.ds(pl.multiple_of(ci * cw, cw), cw)
            for r0 in range(0, tm, rh):
                partial = []
                for res in range(SUBLANES):
                    taps = [k for k in range(CONV_WIDTH) if (first + k) % SUBLANES == res]
                    lo = min(first + k for k in taps) - res
                    hi = max(first + k for k in taps) - res
                    shifted = ext_ref[r0 + lo + res:r0 + hi + res + rh, cs]
                    acc = None
                    for k in taps:
                        off = first + k - res - lo
                        prod = wdw_ref[k:k + 1, cs] * shifted[off:off + rh]
                        acc = prod if acc is None else acc + prod
                    partial.append(acc)
                while len(partial) > 1:
                    partial = [a + b for a, b in zip(partial[0::2], partial[1::2])]
                conv_ref[r0:r0 + rh, cs] = partial[0]
            return carry

        lax.fori_loop(0, d // cw, chan_chunk, 0)
        u = conv_ref[...] + bdw_ref[...]
        u = _layer_norm_rows(u) * lng_ref[...] + lnb_ref[...]
        act_ref[...] = (u * _sigmoid(u)).astype(BF16)

    o_ref[...] = jnp.dot(act_ref[...], wpw_ref[...], preferred_element_type=F32).astype(o_ref.dtype)


def _conv_pw(u, w_dw, b_dw, ln_g, ln_b, w_pw, *, tm=256, tn=2048, cw=256, rh=64):
    s, d = u.shape
    n = w_pw.shape[1]
    ratio = tm // CONV_HALO
    vec = pl.BlockSpec((1, d), lambda i, j: (0, 0))
    return pl.pallas_call(
        functools.partial(_conv_pw_kernel, tm=tm, cw=cw, rh=rh),
        grid=(s // tm, n // tn),
        in_specs=[
            pl.BlockSpec((CONV_HALO, d), lambda i, j: (jnp.maximum(i * ratio - 1, 0), 0)),
            pl.BlockSpec((tm, d), lambda i, j: (i, 0)),
            pl.BlockSpec((CONV_WIDTH, d), lambda i, j: (0, 0)),
            vec, vec, vec,
            pl.BlockSpec((d, tn), lambda i, j: (0, j)),
        ],
        out_specs=pl.BlockSpec((tm, tn), lambda i, j: (i, j)),
        out_shape=jax.ShapeDtypeStruct((s, n), BF16),
        scratch_shapes=[
            pltpu.VMEM((CONV_HALO + tm, d), F32),
            pltpu.VMEM((tm, d), F32),
            pltpu.VMEM((tm, d), BF16),
        ],
        compiler_params=_params("parallel", "arbitrary"),
        name="conv_pw",
    )(u, u, w_dw, b_dw, ln_g, ln_b, w_pw)


def _rms_rows(x, g):
    return x * lax.rsqrt(jnp.mean(x * x, axis=-1, keepdims=True) + RMS_EPS) * g


_NT = (((1,), (1,)), ((), ()))


def _qproj_kernel(cq_ref, g_ref, wt_ref, wrot_t_ref, cos_t_ref, sin_t_ref, o_ref, *, scale):
    cn = _rms_rows(cq_ref[...], g_ref[...]).astype(BF16)
    main_t = lax.dot_general(wt_ref[...], cn, _NT, preferred_element_type=F32)
    rot_t = lax.dot_general(wrot_t_ref[...], cn, _NT, preferred_element_type=F32)
    cos_t = cos_t_ref[...]
    sin_t = sin_t_ref[...]
    n_nope = N_HEADS * QK_NOPE_DIM
    for h in range(N_HEADS):
        nope = main_t[h * QK_NOPE_DIM:(h + 1) * QK_NOPE_DIM]
        r0 = h * QK_ROPE_DIM
        rope = main_t[n_nope + r0:n_nope + r0 + QK_ROPE_DIM] * cos_t + rot_t[r0:r0 + QK_ROPE_DIM] * sin_t
        o_ref[h, 0:QK_NOPE_DIM, :] = (nope * scale).astype(BF16)
        o_ref[h, QK_NOPE_DIM:QK_HEAD_DIM, :] = (rope * scale).astype(BF16)


def _qproj(lat, g_q, wq_t, wq_rot_t, cos_t, sin_t, *, tm=256):
    s = lat.shape[0]
    scale = QK_HEAD_DIM ** -0.5 * 1.4426950408889634
    return pl.pallas_call(
        functools.partial(_qproj_kernel, scale=scale),
        grid=(s // tm,),
        in_specs=[
            pl.BlockSpec((tm, Q_LORA), lambda i: (i, 0)),
            pl.BlockSpec((1, Q_LORA), lambda i: (0, 0)),
            pl.BlockSpec(wq_t.shape, lambda i: (0, 0)),
            pl.BlockSpec(wq_rot_t.shape, lambda i: (0, 0)),
            pl.BlockSpec((QK_ROPE_DIM, tm), lambda i: (0, i)),
            pl.BlockSpec((QK_ROPE_DIM, tm), lambda i: (0, i)),
        ],
        out_specs=pl.BlockSpec((N_HEADS, QK_HEAD_DIM, tm), lambda i: (0, 0, i)),
        out_shape=jax.ShapeDtypeStruct((N_HEADS, QK_HEAD_DIM, s), BF16),
        compiler_params=_params("parallel"),
        name="q_proj",
    )(lat, g_q, wq_t, wq_rot_t, cos_t, sin_t)


def _kvproj_kernel(ckv_ref, kr_ref, g_ref, wk_ref, wv_t_ref, cos_ref, sin_ref, k_ref, vt_ref):
    cn = _rms_rows(ckv_ref[...], g_ref[...]).astype(BF16)
    kr = kr_ref[:, 0:QK_ROPE_DIM]
    kr_rot = kr_ref[:, QK_ROPE_DIM:2 * QK_ROPE_DIM]
    krope = (kr * cos_ref[...] + kr_rot * sin_ref[...]).astype(BF16)
    k_all = jnp.dot(cn, wk_ref[...], preferred_element_type=F32)
    v_t = lax.dot_general(wv_t_ref[...], cn, _NT, preferred_element_type=F32)
    for h in range(N_HEADS):
        k_ref[h, :, 0:QK_NOPE_DIM] = k_all[:, h * QK_NOPE_DIM:(h + 1) * QK_NOPE_DIM].astype(BF16)
        k_ref[h, :, QK_NOPE_DIM:QK_HEAD_DIM] = krope
        vt_ref[h] = v_t[h * V_HEAD_DIM:(h + 1) * V_HEAD_DIM].astype(BF16)


def _kvproj(lat, g_kv, wk, wv_t, cos, sin, *, tm=256):
    s = lat.shape[0]
    kr_block = (Q_LORA + KV_LORA) // (2 * QK_ROPE_DIM)
    return pl.pallas_call(
        _kvproj_kernel,
        grid=(s // tm,),
        in_specs=[
            pl.BlockSpec((tm, KV_LORA), lambda i: (i, Q_LORA // KV_LORA)),
            pl.BlockSpec((tm, 2 * QK_ROPE_DIM), lambda i: (i, kr_block)),
            pl.BlockSpec((1, KV_LORA), lambda i: (0, 0)),
            pl.BlockSpec(wk.shape, lambda i: (0, 0)),
            pl.BlockSpec(wv_t.shape, lambda i: (0, 0)),
            pl.BlockSpec((tm, QK_ROPE_DIM), lambda i: (i, 0)),
            pl.BlockSpec((tm, QK_ROPE_DIM), lambda i: (i, 0)),
        ],
        out_specs=[
            pl.BlockSpec((N_HEADS, tm, QK_HEAD_DIM), lambda i: (0, i, 0)),
            pl.BlockSpec((N_HEADS, V_HEAD_DIM, tm), lambda i: (0, 0, i)),
        ],
        out_shape=[
            jax.ShapeDtypeStruct((N_HEADS, s, QK_HEAD_DIM), BF16),
            jax.ShapeDtypeStruct((N_HEADS, V_HEAD_DIM, s), BF16),
        ],
        compiler_params=_params("parallel"),
        name="kv_proj",
    )(lat, lat, g_kv, wk, wv_t, cos, sin)


def _flash_kernel(qt_ref, k_ref, vt_ref, o_ref, sa_ref, sb_ref, m_ref, l_ref, acc_ref, *, tq, tk):
    qi = pl.program_id(1)
    qt = qt_ref[0]

    def scores(j, s_ref):
        start = pl.multiple_of(j * tk, tk)
        s_ref[...] = jnp.dot(k_ref[0, pl.ds(start, tk), :], qt, preferred_element_type=F32)

    def consume(j, s_ref, mask):
        s = s_ref[...]
        if mask is not None:
            s = jnp.where(mask, s, -jnp.inf)
        m_prev = m_ref[...]
        m_new = jnp.maximum(m_prev, jnp.max(s, axis=0, keepdims=True))
        alpha = jnp.exp2(m_prev - m_new)
        p = jnp.exp2(s - m_new)
        l_ref[...] = alpha * l_ref[...] + jnp.sum(p, axis=0, keepdims=True)
        start = pl.multiple_of(j * tk, tk)
        pv = jnp.dot(vt_ref[0, :, pl.ds(start, tk)], p.astype(BF16), preferred_element_type=F32)
        acc_ref[...] = alpha * acc_ref[...] + pv
        m_ref[...] = m_new

    m_ref[...] = jnp.full(m_ref.shape, -jnp.inf, F32)
    l_ref[...] = jnp.zeros(l_ref.shape, F32)
    acc_ref[...] = jnp.zeros(acc_ref.shape, F32)

    scores(0, sa_ref)

    def pair(tt, carry):
        j = 2 * tt
        scores(j + 1, sb_ref)
        consume(j, sa_ref, None)
        scores(j + 2, sa_ref)
        consume(j + 1, sb_ref, None)
        return carry

    lax.fori_loop(0, qi, pair, 0)

    key_chunk = lax.broadcasted_iota(jnp.int32, (tk, tq), 0) // CHUNK
    query_chunk = lax.broadcasted_iota(jnp.int32, (tk, tq), 1) // CHUNK
    j_diag = 2 * qi
    scores(j_diag + 1, sb_ref)
    consume(j_diag, sa_ref, key_chunk <= query_chunk)
    consume(j_diag + 1, sb_ref, key_chunk + tk // CHUNK <= query_chunk)

    out_t = acc_ref[...] / l_ref[...]
    o_ref[...] = out_t.T.astype(o_ref.dtype)


def _flash(q_t, k, v_t, *, tq=1024, tk=512):
    assert tq == 2 * tk
    h, _, s = q_t.shape
    return pl.pallas_call(
        functools.partial(_flash_kernel, tq=tq, tk=tk),
        grid=(h, s // tq),
        in_specs=[
            pl.BlockSpec((1, QK_HEAD_DIM, tq), lambda hh, i: (hh, 0, i)),
            pl.BlockSpec((1, s, QK_HEAD_DIM), lambda hh, i: (hh, 0, 0)),
            pl.BlockSpec((1, V_HEAD_DIM, s), lambda hh, i: (hh, 0, 0)),
        ],
        out_specs=pl.BlockSpec((tq, V_HEAD_DIM), lambda hh, i: (i, hh)),
        out_shape=jax.ShapeDtypeStruct((s, h * V_HEAD_DIM), BF16),
        scratch_shapes=[
            pltpu.VMEM((tk, tq), F32),
            pltpu.VMEM((tk, tq), F32),
            pltpu.VMEM((1, tq), F32),
            pltpu.VMEM((1, tq), F32),
            pltpu.VMEM((V_HEAD_DIM, tq), F32),
        ],
        compiler_params=_params("parallel", "arbitrary"),
        name="flash_attn",
    )(q_t, k, v_t)


def _wo_mix_kernel(o_ref, w_ref, gc_ref, gm_ref, yc_ref, out_ref):
    y_mla = jnp.dot(o_ref[...], w_ref[...], preferred_element_type=F32)
    mixed = gc_ref[...].astype(F32) * yc_ref[...].astype(F32) + gm_ref[...].astype(F32) * y_mla
    out_ref[...] = mixed.astype(out_ref.dtype)


def _wo_mix(o, w_o, gates, y_conv, *, tm=1024, tn=512):
    s, d = o.shape
    n = w_o.shape[1]
    nb = n // tn
    return pl.pallas_call(
        _wo_mix_kernel,
        grid=(s // tm, nb),
        in_specs=[
            pl.BlockSpec((tm, d), lambda i, j: (i, 0)),
            pl.BlockSpec((d, tn), lambda i, j: (0, j)),
            pl.BlockSpec((tm, tn), lambda i, j: (i, j)),
            pl.BlockSpec((tm, tn), lambda i, j: (i, j + nb)),
            pl.BlockSpec((tm, tn), lambda i, j: (i, j)),
        ],
        out_specs=pl.BlockSpec((tm, tn), lambda i, j: (i, j)),
        out_shape=jax.ShapeDtypeStruct((s, n), BF16),
        compiler_params=_params("parallel", "parallel"),
        name="wo_mix",
    )(o, w_o, gates, gates, y_conv)


def _resid_ln(x, f, gate, lng, lnb):
    y = DEEPNORM_ALPHA * x + (1.0 + gate) * f
    return _layer_norm_rows(y) * lng + lnb


def _mm_resid_ln_kernel(a_ref, w_ref, x_ref, gate_ref, lng_ref, lnb_ref, o_ref, acc_ref):
    kk = pl.program_id(1)

    @pl.when(kk == 0)
    def _():
        acc_ref[...] = jnp.zeros(acc_ref.shape, F32)

    acc_ref[...] += jnp.dot(a_ref[...], w_ref[...], preferred_element_type=F32)

    @pl.when(kk == pl.num_programs(1) - 1)
    def _():
        o_ref[...] = _resid_ln(x_ref[...], acc_ref[...], gate_ref[...], lng_ref[...], lnb_ref[...])


def _mm_resid_ln(a, w, x, gate, lng, lnb, *, tm=512, tk, name):
    s, kdim = a.shape
    d = w.shape[1]
    vec = pl.BlockSpec((1, d), lambda i, k: (0, 0))
    return pl.pallas_call(
        _mm_resid_ln_kernel,
        grid=(s // tm, kdim // tk),
        in_specs=[
            pl.BlockSpec((tm, tk), lambda i, k: (i, k)),
            pl.BlockSpec((tk, d), lambda i, k: (k, 0)),
            pl.BlockSpec((tm, d), lambda i, k: (i, 0)),
            vec, vec, vec,
        ],
        out_specs=pl.BlockSpec((tm, d), lambda i, k: (i, 0)),
        out_shape=jax.ShapeDtypeStruct((s, d), F32),
        scratch_shapes=[pltpu.VMEM((tm, d), F32)],
        compiler_params=_params("parallel", "arbitrary"),
        name=name,
    )(a, w, x, gate, lng, lnb)


def _router_kernel(x_ref, sc_ref, sh_ref, wr_ref, h_ref, comb_ref, sel_ref):
    h = _layer_norm_rows(x_ref[...]) * (1.0 + sc_ref[...]) + sh_ref[...]
    h_ref[...] = _pack_bf16_pair(h)
    logits = jnp.dot(h, wr_ref[...], preferred_element_type=F32, precision=lax.Precision.HIGHEST)
    lane = lax.broadcasted_iota(jnp.int32, logits.shape, 1)
    neg = -jnp.inf
    l1 = jnp.where(lane < N_EXPERTS, logits, neg)
    v1 = jnp.max(l1, axis=1, keepdims=True)
    i1 = jnp.min(jnp.where(l1 == v1, lane, LANES), axis=1, keepdims=True)
    l2 = jnp.where(lane == i1, neg, l1)
    v2 = jnp.max(l2, axis=1, keepdims=True)
    i2 = jnp.min(jnp.where(l2 == v2, lane, LANES), axis=1, keepdims=True)
    e2 = jnp.exp(v2 - v1)
    w1 = 1.0 / (1.0 + e2)
    w2 = e2 / (1.0 + e2)
    comb_ref[...] = jnp.where(lane == i1, w1, 0.0) + jnp.where(lane == i2, w2, 0.0)
    sel_ref[...] = jnp.where(lane == i1, 1.0, 0.0) + jnp.where(lane == i2, 2.0, 0.0)


def _router(x, sc, sh, wr_pad, *, tm=256):
    s, d = x.shape
    vec = pl.BlockSpec((1, d), lambda i: (0, 0))
    lane_blk = pl.BlockSpec((tm, LANES), lambda i: (i, 0))
    return pl.pallas_call(
        _router_kernel,
        grid=(s // tm,),
        in_specs=[pl.BlockSpec((tm, d), lambda i: (i, 0)), vec, vec,
                  pl.BlockSpec((d, LANES), lambda i: (0, 0))],
        out_specs=[pl.BlockSpec((tm, d // 2), lambda i: (i, 0)), lane_blk, lane_blk],
        out_shape=[jax.ShapeDtypeStruct((s, d // 2), jnp.uint32), jax.ShapeDtypeStruct((s, LANES), F32),
                   jax.ShapeDtypeStruct((s, LANES), F32)],
        compiler_params=_params("parallel"),
        name="router",
    )(x, sc, sh, wr_pad)


def _routing_plan(comb, sel, tm, n_tiles):
    sel8 = sel[:, :N_EXPERTS]
    comb8 = comb[:, :N_EXPERTS]
    chosen = (sel8 > 0).astype(jnp.int32)
    rank = jnp.cumsum(chosen, axis=0) - chosen
    counts = jnp.sum(chosen, axis=0)
    tiles_e = (counts + tm - 1) // tm
    tile_end = jnp.cumsum(tiles_e)
    row_start = (tile_end - tiles_e) * tm
    dest = row_start[None, :] + rank
    pos0 = jnp.sum(jnp.where(sel8 == 1.0, dest, 0), axis=1).astype(jnp.int32)
    pos1 = jnp.sum(jnp.where(sel8 == 2.0, dest, 0), axis=1).astype(jnp.int32)
    w0 = jnp.sum(jnp.where(sel8 == 1.0, comb8, 0.0), axis=1, keepdims=True)
    w1 = jnp.sum(jnp.where(sel8 == 2.0, comb8, 0.0), axis=1, keepdims=True)
    n_valid = tile_end[-1]
    tile_id = jnp.minimum(jnp.arange(n_tiles, dtype=jnp.int32), n_valid - 1)
    tile_expert = jnp.sum((tile_id[:, None] >= tile_end[None, :]).astype(jnp.int32), axis=1)
    meta = jnp.concatenate([tile_expert, n_valid[None]]).astype(jnp.int32)
    tok = jnp.arange(sel.shape[0], dtype=jnp.int32)
    src_token = jnp.zeros((n_tiles * tm,), jnp.int32).at[jnp.concatenate([pos0, pos1])].set(
        jnp.concatenate([tok, tok]), unique_indices=True)
    return pos0, pos1, w0, w1, meta, src_token


_HIGH_HALF = 0xFFFF0000
GATHER_ROW_STRIDE = 37


def _pack_bf16_pair(x):
    n = x.shape[1] // 2
    bits = lax.bitcast_convert_type(x.astype(BF16).astype(F32), jnp.uint32)
    return (bits[:, :n] >> 16) | (bits[:, n:] & jnp.uint32(_HIGH_HALF))


def _unpack_bf16_pair(w):
    lo = lax.bitcast_convert_type(w << 16, F32)
    hi = lax.bitcast_convert_type(w & jnp.uint32(_HIGH_HALF), F32)
    return lo, hi


def _row_copy(src_hbm, dst_hbm, src_row, dst_row, sem):
    return pltpu.make_async_copy(src_hbm.at[pl.ds(src_row, 1)], dst_hbm.at[pl.ds(dst_row, 1)], sem)


def _dispatch_kernel(meta_ref, src_ref, src_next_ref, h_hbm, o_ref, buf, sem, *, tb, n_tiles):
    i = pl.program_id(0)
    n_used = meta_ref[n_tiles]
    slot = i % 2

    def start_gather(idx_ref, dst_slot):
        def issue(q, carry):
            for lane in range(2):
                r = ((2 * q + lane) * GATHER_ROW_STRIDE) & (tb - 1)
                _row_copy(h_hbm, buf.at[dst_slot], idx_ref[0, 0, r], r, sem.at[dst_slot]).start(priority=lane)
            return carry

        lax.fori_loop(0, tb // 2, issue, 0)

    @pl.when(jnp.logical_and(i == 0, n_used > 0))
    def _():
        start_gather(src_ref, 0)

    @pl.when(i + 1 < n_used)
    def _():
        start_gather(src_next_ref, 1 - slot)

    @pl.when(i < n_used)
    def _():
        def drain(r, carry):
            _row_copy(h_hbm, buf.at[slot], 0, 0, sem.at[slot]).wait()
            return carry

        lax.fori_loop(0, tb, drain, 0)
        lo, hi = _unpack_bf16_pair(buf[slot])
        half = lo.shape[1]
        o_ref[:, :half] = lo.astype(o_ref.dtype)
        o_ref[:, half:] = hi.astype(o_ref.dtype)

    @pl.when(i >= n_used)
    def _():
        o_ref[...] = jnp.zeros(o_ref.shape, o_ref.dtype)


def _dispatch(meta, src_token, h, *, tb):
    assert tb & (tb - 1) == 0, "row tile must be a power of two (strided issue order)"
    n_rows = src_token.shape[0]
    dp = h.shape[1]
    d = 2 * dp
    n_tiles = n_rows // tb
    return pl.pallas_call(
        functools.partial(_dispatch_kernel, tb=tb, n_tiles=n_tiles),
        grid_spec=pltpu.PrefetchScalarGridSpec(
            num_scalar_prefetch=1,
            grid=(n_tiles,),
            in_specs=[
                pl.BlockSpec((1, 1, tb), lambda i, m: (i, 0, 0), memory_space=pltpu.SMEM),
                pl.BlockSpec((1, 1, tb), lambda i, m: (jnp.minimum(i + 1, n_tiles - 1), 0, 0),
                             memory_space=pltpu.SMEM),
                pl.BlockSpec(memory_space=pl.ANY),
            ],
            out_specs=pl.BlockSpec((tb, d), lambda i, m: (i, 0)),
            scratch_shapes=[pltpu.VMEM((2, tb, dp), jnp.uint32), pltpu.SemaphoreType.DMA((2,))],
        ),
        out_shape=jax.ShapeDtypeStruct((n_rows, d), BF16),
        compiler_params=_params("arbitrary"),
        name="moe_dispatch",
    )(meta, src_token.reshape(n_tiles, 1, tb), src_token.reshape(n_tiles, 1, tb), h)


def _expert_swiglu_kernel(meta_ref, x_ref, wg_ref, wu_ref, wd_ref, o_ref, wd_bf_ref, wg_bf, wu_bf, *, n_tiles):
    i = pl.program_id(1)

    wd_bf_ref[...] = wd_ref[0].astype(BF16)

    @pl.when(jnp.logical_or(i == 0, meta_ref[i] != meta_ref[jnp.maximum(i - 1, 0)]))
    def _():
        wg_bf[...] = wg_ref[0, 0].astype(BF16)
        wu_bf[...] = wu_ref[0, 0].astype(BF16)

    @pl.when(i < meta_ref[n_tiles])
    def _():
        x = x_ref[...]
        a = jnp.dot(x, wg_bf[...], preferred_element_type=F32)
        b = jnp.dot(x, wu_bf[...], preferred_element_type=F32)
        o_ref[...] = (a * _sigmoid(a) * b).astype(o_ref.dtype)

    @pl.when(i >= meta_ref[n_tiles])
    def _():
        o_ref[...] = jnp.zeros(o_ref.shape, o_ref.dtype)


def _expert_swiglu(meta, xs, wg, wu, wd, *, layer, tm, tn=1024):
    r, d = xs.shape
    n_layers, n_exp, _, f = wg.shape
    n_tiles = r // tm
    n_steps = (f // tn) * n_tiles
    wd_rows = n_exp * f
    rb = next(c for c in (256, 512, 1024, 2048, 4096) if wd_rows % c == 0 and wd_rows // c <= n_steps)
    n_wd_blocks = wd_rows // rb
    w_spec = pl.BlockSpec((1, 1, d, tn), lambda j, i, m: (layer, m[i], 0, j))

    def row_tile(i, m):
        return jnp.minimum(i, m[n_tiles] - 1)

    def wd_block(j, i):
        return jnp.minimum(j * n_tiles + i, n_wd_blocks - 1)

    return pl.pallas_call(
        functools.partial(_expert_swiglu_kernel, n_tiles=n_tiles),
        grid_spec=pltpu.PrefetchScalarGridSpec(
            num_scalar_prefetch=1,
            grid=(f // tn, n_tiles),
            in_specs=[
                pl.BlockSpec((tm, d), lambda j, i, m: (row_tile(i, m), 0)),
                w_spec,
                w_spec,
                pl.BlockSpec((1, rb, d), lambda j, i, m: (layer, wd_block(j, i), 0)),
            ],
            out_specs=[
                pl.BlockSpec((tm, tn), lambda j, i, m: (i, j)),
                pl.BlockSpec((rb, d), lambda j, i, m: (wd_block(j, i), 0)),
            ],
            scratch_shapes=[pltpu.VMEM((d, tn), BF16), pltpu.VMEM((d, tn), BF16)],
        ),
        out_shape=[jax.ShapeDtypeStruct((r, f), BF16), jax.ShapeDtypeStruct((wd_rows, d), BF16)],
        compiler_params=pltpu.CompilerParams(dimension_semantics=("arbitrary", "arbitrary"),
                                             vmem_limit_bytes=EXPERT_VMEM_LIMIT),
        name="expert_swiglu",
    )(meta, xs, wg, wu, wd.reshape(n_layers, wd_rows, d))


def _expert_down_kernel(meta_ref, a_ref, w_ref, o_ref, acc_ref, *, n_tiles):
    i = pl.program_id(0)
    kk = pl.program_id(1)

    @pl.when(i < meta_ref[n_tiles])
    def _():
        @pl.when(kk == 0)
        def _():
            acc_ref[...] = jnp.zeros(acc_ref.shape, F32)

        acc_ref[...] += jnp.dot(a_ref[...], w_ref[0], preferred_element_type=F32)

        @pl.when(kk == pl.num_programs(1) - 1)
        def _():
            o_ref[...] = _pack_bf16_pair(acc_ref[...])

    @pl.when(jnp.logical_and(i >= meta_ref[n_tiles], kk == 0))
    def _():
        o_ref[...] = jnp.zeros(o_ref.shape, o_ref.dtype)


def _expert_down(meta, act, wd, *, tm, tk=1024):
    r, f = act.shape
    d = wd.shape[2]
    n_tiles = r // tm
    nk = f // tk

    def used(i, m):
        return i < m[n_tiles]

    def row_tile(i, m):
        return jnp.minimum(i, m[n_tiles] - 1)

    def k_tile(i, k, m):
        return jnp.where(used(i, m), k, nk - 1)

    return pl.pallas_call(
        functools.partial(_expert_down_kernel, n_tiles=n_tiles),
        grid_spec=pltpu.PrefetchScalarGridSpec(
            num_scalar_prefetch=1,
            grid=(n_tiles, nk),
            in_specs=[
                pl.BlockSpec((tm, tk), lambda i, k, m: (row_tile(i, m), k_tile(i, k, m))),
                pl.BlockSpec((1, tk, d), lambda i, k, m: (m[i], k_tile(i, k, m), 0)),
            ],
            out_specs=pl.BlockSpec((tm, d // 2), lambda i, k, m: (i, 0)),
            scratch_shapes=[pltpu.VMEM((tm, d), F32)],
        ),
        out_shape=jax.ShapeDtypeStruct((r, d // 2), jnp.uint32),
        compiler_params=_params("arbitrary", "arbitrary"),
        name="expert_down",
    )(meta, act, wd)


def _combine_ln_kernel(pos0_ref, pos1_ref, pos0_next_ref, pos1_next_ref, ys_hbm, w0_ref, w1_ref, x_ref,
                       gate_ref, lng_ref, lnb_ref, o_ref, a_buf, b_buf, sem, *, tb):
    i = pl.program_id(0)
    slot = i % 2

    def start_gather(p0_ref, p1_ref, dst_slot):
        def issue(r, carry):
            _row_copy(ys_hbm, a_buf.at[dst_slot], p0_ref[0, 0, r], r, sem.at[dst_slot]).start(priority=0)
            _row_copy(ys_hbm, b_buf.at[dst_slot], p1_ref[0, 0, r], r, sem.at[dst_slot]).start(priority=1)
            return carry

        lax.fori_loop(0, tb, issue, 0)

    @pl.when(i == 0)
    def _():
        start_gather(pos0_ref, pos1_ref, 0)

    @pl.when(i + 1 < pl.num_programs(0))
    def _():
        start_gather(pos0_next_ref, pos1_next_ref, 1 - slot)

    def drain(r, carry):
        _row_copy(ys_hbm, a_buf.at[slot], 0, 0, sem.at[slot]).wait()
        _row_copy(ys_hbm, b_buf.at[slot], 0, 0, sem.at[slot]).wait()
        return carry

    lax.fori_loop(0, tb, drain, 0)
    a_lo, a_hi = _unpack_bf16_pair(a_buf[slot])
    b_lo, b_hi = _unpack_bf16_pair(b_buf[slot])
    w0 = w0_ref[...]
    w1 = w1_ref[...]
    f = jnp.concatenate([w0 * a_lo + w1 * b_lo, w0 * a_hi + w1 * b_hi], axis=1)
    o_ref[...] = _resid_ln(x_ref[...], f, gate_ref[...], lng_ref[...], lnb_ref[...])


def _combine_ln(ys, pos0, pos1, w0, w1, x, gate, lng, lnb, *, tb=256):
    s, d = x.shape
    n_blk = s // tb
    idx_blk = pl.BlockSpec((1, 1, tb), lambda i: (i, 0, 0), memory_space=pltpu.SMEM)
    idx_next = pl.BlockSpec((1, 1, tb), lambda i: (jnp.minimum(i + 1, n_blk - 1), 0, 0), memory_space=pltpu.SMEM)
    vec = pl.BlockSpec((1, d), lambda i: (0, 0))
    col = pl.BlockSpec((tb, 1), lambda i: (i, 0))
    blk = pl.BlockSpec((tb, d), lambda i: (i, 0))
    p0 = pos0.reshape(n_blk, 1, tb)
    p1 = pos1.reshape(n_blk, 1, tb)
    return pl.pallas_call(
        functools.partial(_combine_ln_kernel, tb=tb),
        grid=(n_blk,),
        in_specs=[idx_blk, idx_blk, idx_next, idx_next, pl.BlockSpec(memory_space=pl.ANY),
                  col, col, blk, vec, vec, vec],
        out_specs=blk,
        out_shape=jax.ShapeDtypeStruct((s, d), F32),
        scratch_shapes=[pltpu.VMEM((2, tb, d // 2), jnp.uint32), pltpu.VMEM((2, tb, d // 2), jnp.uint32),
                        pltpu.SemaphoreType.DMA((2,))],
        compiler_params=_params("arbitrary"),
        name="moe_combine_ln",
    )(p0, p1, p0, p1, ys, w0, w1, x, gate, lng, lnb)


def _rotate_half_cols(w):
    half = w.shape[-1] // 2
    return jnp.concatenate([-w[..., half:], w[..., :half]], axis=-1)


def kernel(x, c, positions, w_ada, b_ada, w_in, g_q, w_uq, g_kv, w_ukv, w_o, w_dw, b_dw, conv_ln_g, conv_ln_b, w_pw, w_out, ln1_g, ln1_b, ln2_g, ln2_b, w_ff_gate, w_ff_up, w_ff_down, w_router, w_e_gate, w_e_up, w_e_down):
    assert x.shape == (1, SEQ, D_MODEL) and c.shape == (1, D_MODEL)
    d = D_MODEL
    xs = x.reshape(SEQ, d)

    mod = _ada(c.reshape(d, 1), w_ada, b_ada).reshape(DEPTH, 6, 1, d)

    inv_freq = ROPE_THETA ** (-jnp.arange(0, QK_ROPE_DIM, 2, dtype=F32) / QK_ROPE_DIM)
    inv_freq2 = jnp.concatenate([inv_freq, inv_freq])
    cos, sin, cos_t, sin_t = _rope_tables(positions.reshape(SEQ), inv_freq2)

    row = lambda v: v.reshape(1, -1)

    for l in range(DEPTH):
        sh1, sc1, g1, sh2, sc2, g2 = (mod[l, t] for t in range(6))
        wl = w_in[l]
        o_kr = Q_LORA + KV_LORA
        o_val = o_kr + QK_ROPE_DIM
        w_kr = wl[:, o_kr:o_val]
        w_lat = jnp.concatenate([wl[:, :o_val], _rotate_half_cols(w_kr)], axis=1).astype(BF16)
        w_val = wl[:, o_val:o_val + d].astype(BF16)
        w_gate = wl[:, o_val + d:o_val + 2 * d].astype(BF16)
        w_brg = wl[:, o_val + 2 * d:].astype(BF16)

        lat, u, gates = _in_proj(xs, sc1, sh1, w_lat, w_val, w_gate, w_brg)

        y_conv = _conv_pw(u, w_dw[l], row(b_dw[l]), row(conv_ln_g[l]), row(conv_ln_b[l]), w_pw[l].astype(BF16))

        wq = w_uq[l].reshape(Q_LORA, N_HEADS, QK_HEAD_DIM)
        wq_rope = wq[..., QK_NOPE_DIM:]
        wq_t = jnp.concatenate([wq[..., :QK_NOPE_DIM].reshape(Q_LORA, -1), wq_rope.reshape(Q_LORA, -1)], axis=1).T
        wq_rot_t = _rotate_half_cols(wq_rope).reshape(Q_LORA, -1).T
        wkv = w_ukv[l].reshape(KV_LORA, N_HEADS, QK_NOPE_DIM + V_HEAD_DIM)
        wk = wkv[..., :QK_NOPE_DIM].reshape(KV_LORA, -1)
        wv_t = wkv[..., QK_NOPE_DIM:].reshape(KV_LORA, -1).T
        q_t = _qproj(lat, row(g_q[l]), wq_t.astype(BF16), wq_rot_t.astype(BF16), cos_t, sin_t)
        k, v_t = _kvproj(lat, row(g_kv[l]), wk.astype(BF16), wv_t.astype(BF16), cos, sin)
        o = _flash(q_t, k, v_t)
        mixed = _wo_mix(o, w_o[l].astype(BF16), gates, y_conv)
        xs = _mm_resid_ln(mixed, w_out[l].astype(BF16), xs, g1, row(ln1_g[l]), row(ln1_b[l]), tk=d, name="out_proj_ln")

        i = l // 2
        if l % 2 == 0:
            act = _ffn_swiglu(xs, sc2, sh2, w_ff_gate[i].astype(BF16), w_ff_up[i].astype(BF16))
            xs = _mm_resid_ln(act, w_ff_down[i].astype(BF16), xs, g2, row(ln2_g[l]), row(ln2_b[l]), tk=D_FF_DENSE // 4, name="ffn_down_ln")
        else:
            wr_pad = jnp.zeros((d, LANES), F32).at[:, :N_EXPERTS].set(w_router[i])
            h2, comb, sel = _router(xs, sc2, sh2, wr_pad)
            tm_e = EXPERT_ROW_TILE
            n_tiles = (2 * SEQ) // tm_e + N_EXPERTS
            pos0, pos1, w0, w1, meta, src_token = _routing_plan(comb, sel, tm_e, n_tiles)
            x_sorted = _dispatch(meta, src_token, h2, tb=tm_e)
            act, wd_bf = _expert_swiglu(meta, x_sorted, w_e_gate, w_e_up, w_e_down, layer=i, tm=tm_e)
            y_sorted = _expert_down(meta, act, wd_bf.reshape(N_EXPERTS, D_FF_EXPERT, d), tm=tm_e)
            xs = _combine_ln(y_sorted, pos0, pos1, w0, w1, xs, g2, row(ln2_g[l]), row(ln2_b[l]))

    return xs.reshape(1, SEQ, D_MODEL)
```
